```python
import math
import jax, jax.numpy as jnp
from jax import lax
import numpy as np

D_MODEL = 2048
BATCH = 4
SEQ = 4096
DEPTH = 2

N_EVEN = (DEPTH + 1) // 2
N_ODD = DEPTH // 2
ATTN_HEADS = 8
HEAD_DIM = 128
ATTN_WIDTH = ATTN_HEADS * HEAD_DIM
DILATED_BRANCHES = ((128, 1), (512, 4), (2048, 16))
HYENA_WIDTH = D_MODEL - ATTN_WIDTH
HYENA_POS_DIM = 33
HYENA_FILTER_ORDER = 64
HYENA_SHORT_WIDTH = 3
HYENA_DECAY_FAST = 0.3
HYENA_DECAY_SLOW = 1.5
HYENA_DECAY_TARGET = 1e-2
HYENA_FILTER_GAIN = 0.05
IN_WIDTH = 3 * ATTN_WIDTH + 3 * HYENA_WIDTH
POOL_WINDOWS = (2, 4, 8, 16)
POOL_GROUP = D_MODEL // len(POOL_WINDOWS)
N_EXPERTS = 16
EXPERT_FF = 2048
EC_CAPACITY_FACTOR = 2
DEEPNORM_ALPHA = (2 * DEPTH) ** 0.25
DEEPNORM_BETA = (8 * DEPTH) ** -0.25
LN_EPS = 1e-5
NEG_BIG = -1e30

kernel_name = "hybrid_dilated_hyena_pool_ecmoe_encoder"

F32 = jnp.float32


def layer_norm(x, g, b):
    xf = x.astype(F32)
    mu = jnp.mean(xf, axis=-1, keepdims=True)
    var = jnp.mean(jnp.square(xf - mu), axis=-1, keepdims=True)
    y = (xf - mu) * lax.rsqrt(var + LN_EPS) * g.astype(F32) + b.astype(F32)
    return y.astype(x.dtype)


def alibi_slopes(n_heads):
    return 2.0 ** (-(8.0 / n_heads) * jnp.arange(1, n_heads + 1, dtype=F32))


def dilated_branch(q, k, v, slopes, window, dilation):
    B, H, S, hd = q.shape
    n = window // (2 * dilation)
    Lc = S // dilation
    nb = -(-Lc // n)
    Lp = nb * n

    def to_classes(t):
        return t.reshape(B, H, Lc, dilation, hd).transpose(0, 1, 3, 2, 4)

    qc, kc, vc = to_classes(q), to_classes(k), to_classes(v)
    qb = jnp.pad(qc, ((0, 0), (0, 0), (0, 0), (0, Lp - Lc), (0, 0))).reshape(B, H, dilation, nb, n, hd)

    def neighbour_blocks(t):
        tb = jnp.pad(t, ((0, 0), (0, 0), (0, 0), (n, n + Lp - Lc), (0, 0))).reshape(B, H, dilation, nb + 2, n, hd)
        return jnp.concatenate([tb[:, :, :, :-2], tb[:, :, :, 1:-1], tb[:, :, :, 2:]], axis=4)

    kb, vb = neighbour_blocks(kc), neighbour_blocks(vc)
    s = jnp.einsum('bhrjqe,bhrjke->bhrjqk', qb, kb) * (1.0 / math.sqrt(hd))
    qi = jnp.arange(n)[:, None]
    ki = jnp.arange(3 * n)[None, :]
    rel = ki - n - qi
    key_pos = jnp.arange(nb)[:, None, None] * n - n + ki[None]
    valid = (jnp.abs(rel)[None] <= n) & (key_pos >= 0) & (key_pos < Lc)
    dist = (jnp.abs(rel) * dilation).astype(F32)
    bias = -slopes[:, None, None, None, None] * dist
    s = jnp.where(valid, s + bias, NEG_BIG)
    m = jnp.max(s, axis=-1, keepdims=True)
    p = jnp.exp(s - m)
    l = jnp.sum(p, axis=-1, keepdims=True)
    o = jnp.einsum('bhrjqk,bhrjke->bhrjqe', p, vb) / l
    lse = (m + jnp.log(l))[..., 0]

    o = o.reshape(B, H, dilation, Lp, hd)[:, :, :, :Lc]
    o = jnp.moveaxis(o, 2, 3).reshape(B, H, S, hd)
    lse = lse.reshape(B, H, dilation, Lp)[:, :, :, :Lc]
    lse = jnp.moveaxis(lse, 2, 3).reshape(B, H, S)
    return o, lse


def dilated_attention(q, k, v):
    slopes = alibi_slopes(q.shape[1])
    outs, lses = [], []
    for window, dilation in DILATED_BRANCHES:
        o, lse = dilated_branch(q, k, v, slopes, window, dilation)
        outs.append(o)
        lses.append(lse)
    w = jax.nn.softmax(jnp.stack(lses), axis=0)
    return jnp.einsum('nbhs,nbhse->bhse', w, jnp.stack(outs))


def short_conv(u, w, b):
    C = u.shape[-1]
    y = lax.conv_general_dilated(u, w[:, None, :].astype(u.dtype), window_strides=(1,),
                                 padding=((HYENA_SHORT_WIDTH // 2, HYENA_SHORT_WIDTH // 2),),
                                 dimension_numbers=('NWC', 'WIO', 'NWC'), feature_group_count=C)
    return y + b.astype(u.dtype)


def hyena_filters(L, w1, b1, w2, b2, w3, b3, w4, freq):
    t = jnp.linspace(0.0, 1.0, L, dtype=F32)[:, None]
    bands = (HYENA_POS_DIM - 1) // 2
    w_ang = 2.0 * math.pi * jnp.arange(L, dtype=F32)[:, None] / L
    f = jnp.linspace(1e-4, bands - 1, bands, dtype=F32)[None, :]
    z = jnp.concatenate([t, jnp.cos(f * w_ang), -jnp.sin(f * w_ang)], axis=-1)
    fr = freq.astype(F32)
    h = jnp.sin(fr * (z @ w1.astype(F32) + b1.astype(F32)))
    h = jnp.sin(fr * (h @ w2.astype(F32) + b2.astype(F32)))
    h = jnp.sin(fr * (h @ w3.astype(F32) + b3.astype(F32)))
    h = (h @ w4.astype(F32)).reshape(L, 2, HYENA_WIDTH)
    max_decay = math.log(HYENA_DECAY_TARGET) / HYENA_DECAY_FAST
    min_decay = math.log(HYENA_DECAY_TARGET) / HYENA_DECAY_SLOW
    deltas = jnp.linspace(min_decay, max_decay, HYENA_WIDTH, dtype=F32)
    decay = jnp.exp(-t * jnp.abs(deltas)[None, :])
    h = h * decay[:, None, :]
    return h[:, 0], h[:, 1]


def bidirectional_long_conv(u, h_fwd, h_bwd):
    L = u.shape[1]
    k_circ = jnp.concatenate([h_fwd, jnp.zeros_like(h_fwd[:1]), h_bwd[:0:-1]], axis=0)
    kf = jnp.fft.rfft(k_circ, axis=0)
    uf = jnp.fft.rfft(u, n=2 * L, axis=1)
    return jnp.fft.irfft(uf * kf[None], n=2 * L, axis=1)[:, :L]


def parallel_attention_hyena(x, w_in, w_out, conv_w, conv_b, fw1, fb1, fw2, fb2, fw3, fb3, fw4,
                             sin_freq, hy_bias):
    B, S, _ = x.shape
    proj = x @ w_in
    qkv = proj[..., :3 * ATTN_WIDTH].reshape(B, S, 3, ATTN_HEADS, HEAD_DIM).astype(F32)
    q = qkv[:, :, 0].transpose(0, 2, 1, 3)
    k = qkv[:, :, 1].transpose(0, 2, 1, 3)
    v = qkv[:, :, 2].transpose(0, 2, 1, 3)
    attn = dilated_attention(q, k, v).transpose(0, 2, 1, 3).reshape(B, S, ATTN_WIDTH)

    hy = short_conv(proj[..., 3 * ATTN_WIDTH:], conv_w, conv_b)
    x0 = hy[..., :HYENA_WIDTH]
    x1 = hy[..., HYENA_WIDTH:2 * HYENA_WIDTH]
    hv = hy[..., 2 * HYENA_WIDTH:]
    h_fwd, h_bwd = hyena_filters(S, fw1, fb1, fw2, fb2, fw3, fb3, fw4, sin_freq)
    z = (hv * x1).astype(F32)
    z = bidirectional_long_conv(z, h_fwd, h_bwd) + z * hy_bias.astype(F32)
    hyena = x0.astype(F32) * z

    mixed = jnp.concatenate([attn, hyena], axis=-1).astype(x.dtype)
    return mixed @ w_out


def multiscale_pool(x, pool_w, pool_scale):
    B, S, D = x.shape
    xf = x.astype(F32)
    cs = jnp.concatenate([jnp.zeros((B, 1, D), F32), lax.cumsum(xf, axis=1)], axis=1)
    pos = jnp.arange(S)
    outs = []
    for g, win in enumerate(POOL_WINDOWS):
        sl = slice(g * POOL_GROUP, (g + 1) * POOL_GROUP)
        lo = jnp.clip(pos - win // 2, 0, S)
        hi = jnp.clip(pos + win // 2, 0, S)
        csg = cs[..., sl]
        mean = (csg[:, hi] - csg[:, lo]) / (hi - lo).astype(F32)[None, :, None]
        outs.append(jnp.einsum('bsc,cd->bsd', mean - xf[..., sl], pool_w[g].astype(F32)))
    return (jnp.concatenate(outs, axis=-1) * pool_scale.astype(F32)).astype(x.dtype)


def expert_choice_ffn(x, router_w, w1, w3, w2):
    B, S, D = x.shape
    cap = EC_CAPACITY_FACTOR * S // N_EXPERTS
    aff = jax.nn.softmax((x @ router_w).astype(F32), axis=-1)
    gate, idx = lax.top_k(aff.transpose(0, 2, 1), cap)
    bidx = jnp.arange(B)[:, None, None]
    xs = x[bidx, idx]
    h = jax.nn.silu(jnp.einsum('becd,edf->becf', xs, w1)) * jnp.einsum('becd,edf->becf', xs, w3)
    y = jnp.einsum('becf,efd->becd', h, w2) * gate[..., None].astype(x.dtype)
    return jnp.zeros_like(x).at[bidx, idx].add(y)


def setup_inputs(seed: int = 0) -> dict:
    key = jax.random.key(seed)
    ks = jax.random.split(key, 24)

    def nrm(i, shape, scale):
        return jax.random.normal(ks[i], shape, F32) * scale

    ne, no = N_EVEN, N_ODD
    hw, fo, pd = HYENA_WIDTH, HYENA_FILTER_ORDER, HYENA_POS_DIM
    beta = DEEPNORM_BETA
    return {
        "x": nrm(0, (BATCH, SEQ, D_MODEL), 1.0),
        "mix_w_in": nrm(1, (ne, D_MODEL, IN_WIDTH), D_MODEL ** -0.5),
        "mix_w_out": nrm(2, (ne, D_MODEL, D_MODEL), beta * D_MODEL ** -0.5),
        "hy_conv_w": nrm(3, (ne, HYENA_SHORT_WIDTH, 3 * hw), HYENA_SHORT_WIDTH ** -0.5),
        "hy_conv_b": nrm(4, (ne, 3 * hw), 0.02),
        "hy_ffn_w1": nrm(5, (ne, pd, fo), pd ** -0.5),
        "hy_ffn_b1": nrm(6, (ne, fo), 0.02),
        "hy_ffn_w2": nrm(7, (ne, fo, fo), fo ** -0.5),
        "hy_ffn_b2": nrm(8, (ne, fo), 0.02),
        "hy_ffn_w3": nrm(9, (ne, fo, fo), fo ** -0.5),
        "hy_ffn_b3": nrm(10, (ne, fo), 0.02),
        "hy_ffn_w4": nrm(11, (ne, fo, 2 * hw), HYENA_FILTER_GAIN * fo ** -0.5),
        "hy_sin_freq": 1.0 + nrm(12, (ne, fo), 0.05),
        "hy_bias": nrm(13, (ne, hw), 0.1),
        "pool_w": nrm(14, (no, len(POOL_WINDOWS), POOL_GROUP, POOL_GROUP), beta * POOL_GROUP ** -0.5),
        "pool_scale": 1.0 + nrm(15, (no, D_MODEL), 0.1),
        "ln_mix_g": 1.0 + nrm(16, (DEPTH, D_MODEL), 0.02),
        "ln_mix_b": nrm(17, (DEPTH, D_MODEL), 0.02),
        "ln_ffn_g": 1.0 + nrm(18, (DEPTH, D_MODEL), 0.02),
        "ln_ffn_b": nrm(19, (DEPTH, D_MODEL), 0.02),
        "router_w": nrm(20, (DEPTH, D_MODEL, N_EXPERTS), D_MODEL ** -0.5),
        "exp_w1": nrm(21, (DEPTH, N_EXPERTS, D_MODEL, EXPERT_FF), D_MODEL ** -0.5),
        "exp_w3": nrm(22, (DEPTH, N_EXPERTS, D_MODEL, EXPERT_FF), D_MODEL ** -0.5),
        "exp_w2": nrm(23, (DEPTH, N_EXPERTS, EXPERT_FF, D_MODEL), beta * EXPERT_FF ** -0.5),
    }


def reference(x, mix_w_in, mix_w_out, hy_conv_w, hy_conv_b, hy_ffn_w1, hy_ffn_b1, hy_ffn_w2,
              hy_ffn_b2, hy_ffn_w3, hy_ffn_b3, hy_ffn_w4, hy_sin_freq, hy_bias, pool_w, pool_scale,
              ln_mix_g, ln_mix_b, ln_ffn_g, ln_ffn_b, router_w, exp_w1, exp_w3, exp_w2):
    for layer in range(DEPTH):
        i = layer // 2
        if layer % 2 == 0:
            mixed = parallel_attention_hyena(
                x, mix_w_in[i], mix_w_out[i], hy_conv_w[i], hy_conv_b[i],
                hy_ffn_w1[i], hy_ffn_b1[i], hy_ffn_w2[i], hy_ffn_b2[i], hy_ffn_w3[i], hy_ffn_b3[i],
                hy_ffn_w4[i], hy_sin_freq[i], hy_bias[i])
        else:
            mixed = multiscale_pool(x, pool_w[i], pool_scale[i])
        x = layer_norm(DEEPNORM_ALPHA * x + mixed, ln_mix_g[layer], ln_mix_b[layer])
        moe = expert_choice_ffn(x, router_w[layer], exp_w1[layer], exp_w3[layer], exp_w2[layer])
        x = layer_norm(DEEPNORM_ALPHA * x + moe, ln_ffn_g[layer], ln_ffn_b[layer])
    return x
```

```python
import functools
import math

import numpy as np
import jax
import jax.numpy as jnp
from jax import lax
from jax.experimental import pallas as pl
from jax.experimental.pallas import tpu as pltpu

F32 = jnp.float32
BF16 = jnp.bfloat16
I32 = jnp.int32

LANES = 128
SUBLANES = 8
VMEM_LIMIT_BYTES = 56 * 1024 * 1024

ATTN_HEADS = 8
HEAD_DIM = 128
DILATED_BRANCHES = ((128, 1), (512, 4), (2048, 16))
POOL_WINDOWS = (2, 4, 8, 16)
N_EXPERTS = 16
EC_CAPACITY_FACTOR = 2
HYENA_POS_DIM = 33
HYENA_DECAY_FAST = 0.3
HYENA_DECAY_SLOW = 1.5
HYENA_DECAY_TARGET = 1e-2
LN_EPS = 1e-5
NEG_BIG = -1e30


def _cparams(*sem):
    return pltpu.CompilerParams(dimension_semantics=sem, vmem_limit_bytes=VMEM_LIMIT_BYTES)


DOT_TM = 256
DOT_TN = 512


def _dot_tiles(a_ref, b_ref, emit, cast=False):
    m, n = a_ref.shape[0], b_ref.shape[1]
    tm, tn = min(DOT_TM, m), min(DOT_TN, n)
    for r in range(0, m, tm):
        a = a_ref[r:r + tm, :]
        a = a.astype(BF16) if cast else a
        for c in range(0, n, tn):
            emit(slice(r, r + tm), slice(c, c + tn), jnp.dot(a, b_ref[:, c:c + tn], preferred_element_type=F32))


def _mm_kernel(x_ref, w_ref, o_ref, wb_ref):
    @pl.when(pl.program_id(1) == 0)
    def _():
        for r in range(0, w_ref.shape[0], DOT_TM):
            wb_ref[r:r + DOT_TM, :] = w_ref[r:r + DOT_TM, :].astype(BF16)

    def emit(rows, cols, v):
        o_ref[rows, cols] = v.astype(o_ref.dtype)

    _dot_tiles(x_ref, wb_ref, emit, cast=True)


def _matmul(x, w, out_dtype, tm, tn):
    m, k = x.shape
    n = w.shape[1]
    return pl.pallas_call(
        _mm_kernel,
        grid=(n // tn, m // tm),
        in_specs=[pl.BlockSpec((tm, k), lambda j, i: (i, 0)), pl.BlockSpec((k, tn), lambda j, i: (0, j))],
        out_specs=pl.BlockSpec((tm, tn), lambda j, i: (i, j)),
        out_shape=jax.ShapeDtypeStruct((m, n), out_dtype),
        scratch_shapes=[pltpu.VMEM((k, tn), BF16)],
        compiler_params=_cparams("arbitrary", "arbitrary"),
        name="matmul",
    )(x, w)


def _layer_norm_rows(v, g, b):
    mu = jnp.mean(v, axis=-1, keepdims=True)
    c = v - mu
    var = jnp.mean(c * c, axis=-1, keepdims=True)
    return c * lax.rsqrt(var + LN_EPS) * g + b


def _router_affinity(x1, rw_ref, n_experts):
    logits = jnp.dot(x1, rw_ref[...], preferred_element_type=F32, precision=lax.Precision.HIGHEST)
    lane = lax.broadcasted_iota(I32, logits.shape, 1)
    logits = jnp.where(lane < n_experts, logits, NEG_BIG)
    mx = jnp.max(logits, axis=-1, keepdims=True)
    e = jnp.exp(logits - mx)
    return e / jnp.sum(e, axis=-1, keepdims=True)


def _ln_router_kernel(alpha, n_experts, x_ref, m_ref, g_ref, b_ref, rw_ref, xe_ref, afft_ref):
    d = x_ref.shape[1]
    x1 = _layer_norm_rows(alpha * x_ref[...] + m_ref[...].astype(F32), g_ref[...], b_ref[...])
    aff = _router_affinity(x1, rw_ref, n_experts)
    xe_ref[:, :d] = x1
    xe_ref[:, d:] = aff
    afft_ref[...] = jnp.transpose(aff)[:n_experts, :]


def _ln_router(x, mixed, g, b, rw_pad, alpha, n_experts, tm=256):
    t, d = x.shape
    return pl.pallas_call(
        functools.partial(_ln_router_kernel, alpha, n_experts),
        grid=(t // tm,),
        in_specs=[
            pl.BlockSpec((tm, d), lambda i: (i, 0)),
            pl.BlockSpec((tm, d), lambda i: (i, 0)),
            pl.BlockSpec((1, d), lambda i: (0, 0)),
            pl.BlockSpec((1, d), lambda i: (0, 0)),
            pl.BlockSpec((d, LANES), lambda i: (0, 0)),
        ],
        out_specs=[pl.BlockSpec((tm, d + LANES), lambda i: (i, 0)), pl.BlockSpec((n_experts, tm), lambda i: (0, i))],
        out_shape=[jax.ShapeDtypeStruct((t, d + LANES), F32), jax.ShapeDtypeStruct((n_experts, t), F32)],
        compiler_params=_cparams("arbitrary"),
        name="ln_router",
    )(x, mixed, g, b, rw_pad)


ATTN_QBLK = 256
ATTN_HALF = 64


def _attn_branch(slope, dil, lc, q_cm, k_cm, v_cm, o_cm, l_cm, seq):
    win = min(ATTN_QBLK + 2 * ATTN_HALF, lc)
    blocks_per_class = lc // ATTN_QBLK
    scale = 1.0 / math.sqrt(HEAD_DIM)
    diff = lax.broadcasted_iota(I32, (ATTN_QBLK, win), 1) - lax.broadcasted_iota(I32, (ATTN_QBLK, win), 0)

    def body(blk, carry):
        cls = blk // blocks_per_class
        q0 = (blk % blocks_per_class) * ATTN_QBLK
        w0 = jnp.clip(q0 - ATTN_HALF, 0, lc - win)
        qs = pl.multiple_of(cls * lc + q0, ATTN_HALF)
        ks = pl.multiple_of(cls * lc + w0, ATTN_HALF)
        q = q_cm[pl.ds(qs, ATTN_QBLK), :]
        k = k_cm[pl.ds(ks, win), :]
        v = v_cm[pl.ds(ks, win), :]
        s = lax.dot_general(q, k, (((1,), (1,)), ((), ())), preferred_element_type=F32) * scale
        rel = jnp.abs(diff + (w0 - q0))
        s = jnp.where(rel <= ATTN_HALF, s - (slope * dil) * rel.astype(F32), NEG_BIG)
        m = jnp.max(s, axis=-1, keepdims=True)
        p = jnp.exp(s - m)
        l = jnp.sum(p, axis=-1, keepdims=True)
        o = jnp.dot(p.astype(BF16), v, preferred_element_type=F32) / l
        o_cm[pl.ds(qs, ATTN_QBLK), :] = o
        l_cm[pl.ds(qs, ATTN_QBLK), :] = jnp.broadcast_to(m + jnp.log(l), (ATTN_QBLK, HEAD_DIM))
        return carry

    lax.fori_loop(0, seq // ATTN_QBLK, body, 0)


def _attn_kernel(slopes_ref, q_ref, k_ref, v_ref, out_ref, qf, kf, vf, qc, kc, vc, ocm, lcm, onat, lnat):
    seq = q_ref.shape[0]
    slope = slopes_ref[pl.program_id(1)]
    qf[...] = q_ref[...].astype(F32)
    kf[...] = k_ref[...].astype(F32)
    vf[...] = v_ref[...].astype(F32)
    for bi, (window, dil) in enumerate(DILATED_BRANCHES):
        assert window // (2 * dil) == ATTN_HALF
        lc = seq // dil
        if dil == 1:
            _attn_branch(slope, dil, lc, q_ref, k_ref, v_ref, onat.at[bi], lnat.at[bi], seq)
            continue
        for r in range(dil):
            qc[r * lc:(r + 1) * lc, :] = qf[pl.ds(r, lc, stride=dil), :].astype(BF16)
            kc[r * lc:(r + 1) * lc, :] = kf[pl.ds(r, lc, stride=dil), :].astype(BF16)
            vc[r * lc:(r + 1) * lc, :] = vf[pl.ds(r, lc, stride=dil), :].astype(BF16)
        _attn_branch(slope, dil, lc, qc, kc, vc, ocm, lcm, seq)
        for r in range(dil):
            onat[bi, pl.ds(r, lc, stride=dil), :] = ocm[r * lc:(r + 1) * lc, :]
            lnat[bi, pl.ds(r, lc, stride=dil), :] = lcm[r * lc:(r + 1) * lc, :]
    lses = [lnat[bi] for bi in range(len(DILATED_BRANCHES))]
    mx = functools.reduce(jnp.maximum, lses)
    ws = [jnp.exp(l - mx) for l in lses]
    num = sum(w * onat[bi] for bi, w in enumerate(ws))
    out_ref[...] = (num / sum(ws)).astype(out_ref.dtype)


def _dilated_attention(proj, batch, seq):
    t = proj.shape[0]
    slopes = jnp.asarray(2.0 ** (-(8.0 / ATTN_HEADS) * np.arange(1, ATTN_HEADS + 1)), F32)
    blk = (seq, HEAD_DIM)
    nb = len(DILATED_BRANCHES)
    return pl.pallas_call(
        _attn_kernel,
        grid_spec=pltpu.PrefetchScalarGridSpec(
            num_scalar_prefetch=1,
            grid=(batch, ATTN_HEADS),
            in_specs=[
                pl.BlockSpec(blk, lambda b, h, s: (b, h)),
                pl.BlockSpec(blk, lambda b, h, s: (b, ATTN_HEADS + h)),
                pl.BlockSpec(blk, lambda b, h, s: (b, 2 * ATTN_HEADS + h)),
            ],
            out_specs=pl.BlockSpec(blk, lambda b, h, s: (b, h)),
            scratch_shapes=[pltpu.VMEM(blk, F32)] * 3 + [pltpu.VMEM(blk, BF16)] * 3 + [pltpu.VMEM(blk, F32)] * 2
            + [pltpu.VMEM((nb,) + blk, F32)] * 2,
        ),
        out_shape=jax.ShapeDtypeStruct((t, ATTN_HEADS * HEAD_DIM), BF16),
        compiler_params=_cparams("arbitrary", "arbitrary"),
        name="dilated_attention",
    )(slopes, proj, proj, proj)


HYENA_ORDER_PAD = LANES


def _filter_kernel(z_ref, w1, b1, w2, b2, w3, b3, w4, fr, absdelta_ref, h_ref):
    hi = lax.Precision.HIGHEST
    f = fr[...]
    z = z_ref[...]
    h = jnp.sin(f * (jnp.dot(z, w1[...], preferred_element_type=F32, precision=hi) + b1[...]))
    h = jnp.sin(f * (jnp.dot(h, w2[...], preferred_element_type=F32, precision=hi) + b2[...]))
    h = jnp.sin(f * (jnp.dot(h, w3[...], preferred_element_type=F32, precision=hi) + b3[...]))
    h = jnp.dot(h, w4[...], preferred_element_type=F32, precision=hi)
    t = z[:, 0:1]
    h = h * jnp.exp(-t * absdelta_ref[...])
    c = h.shape[1] // 2
    row = lax.broadcasted_iota(I32, h.shape, 0) + pl.program_id(0) * h.shape[0]
    col = lax.broadcasted_iota(I32, h.shape, 1)
    h_ref[...] = jnp.where((row == 0) & (col >= c), 0.0, h).astype(h_ref.dtype)


def _position_features(seq):
    t = np.linspace(0.0, 1.0, seq)[:, None]
    bands = (HYENA_POS_DIM - 1) // 2
    w_ang = 2.0 * np.pi * np.arange(seq)[:, None] / seq
    f = np.linspace(1e-4, bands - 1, bands)[None, :]
    z = np.concatenate([t, np.cos(f * w_ang), -np.sin(f * w_ang)], axis=-1)
    return np.pad(z, ((0, 0), (0, HYENA_ORDER_PAD - z.shape[1]))).astype(np.float32)


def _pad2(a, rows, cols):
    return jnp.pad(a, ((0, rows - a.shape[0]), (0, cols - a.shape[1])))


def _hyena_filters(seq, w1, b1, w2, b2, w3, b3, w4, freq, tl=512):
    p = HYENA_ORDER_PAD
    c2 = w4.shape[1]
    c = c2 // 2
    max_decay = math.log(HYENA_DECAY_TARGET) / HYENA_DECAY_FAST
    min_decay = math.log(HYENA_DECAY_TARGET) / HYENA_DECAY_SLOW
    absdelta = np.abs(np.linspace(min_decay, max_decay, c))
    absdelta = jnp.asarray(np.concatenate([absdelta, absdelta])[None, :], F32)
    args = (
        jnp.asarray(_position_features(seq)),
        _pad2(w1, p, p), _pad2(b1[None, :], 1, p), _pad2(w2, p, p), _pad2(b2[None, :], 1, p),
        _pad2(w3, p, p), _pad2(b3[None, :], 1, p), _pad2(w4, p, c2), _pad2(freq[None, :], 1, p), absdelta,
    )
    full = lambda a: pl.BlockSpec(a.shape, lambda i: (0, 0))
    return pl.pallas_call(
        _filter_kernel,
        grid=(seq // tl,),
        in_specs=[pl.BlockSpec((tl, p), lambda i: (i, 0))] + [full(a) for a in args[1:]],
        out_specs=pl.BlockSpec((tl, c2), lambda i: (i, 0)),
        out_shape=jax.ShapeDtypeStruct((seq, c2), BF16),
        compiler_params=_cparams("arbitrary"),
        name="hyena_filters",
    )(*args)


def _shift_rows(u, edge_row, down):
    n = u.shape[0]
    row = lax.broadcasted_iota(I32, u.shape, 0)
    if down:
        return jnp.where(row == 0, edge_row, pltpu.roll(u, 1, 0))
    return jnp.where(row == n - 1, edge_row, pltpu.roll(u, n - 1, 0))


def _hyena_pre_kernel(tiles_per_seq, cur_ref, prev_ref, next_ref, w_ref, b_ref, z_ref, x0_ref):
    i = pl.program_id(0) % tiles_per_seq
    u = cur_ref[...].astype(F32)
    prev_row = jnp.where(i == 0, 0.0, prev_ref[SUBLANES - 1:SUBLANES, :].astype(F32))
    next_row = jnp.where(i == tiles_per_seq - 1, 0.0, next_ref[0:1, :].astype(F32))
    y = (w_ref[0:1, :] * _shift_rows(u, prev_row, True) + w_ref[1:2, :] * u
         + w_ref[2:3, :] * _shift_rows(u, next_row, False) + b_ref[...])
    c = y.shape[1] // 3
    x0_ref[...] = y[:, :c].astype(x0_ref.dtype)
    z_ref[...] = (y[:, 2 * c:] * y[:, c:2 * c]).astype(z_ref.dtype)


def _hyena_pre(proj, conv_w, conv_b, seq, ts=256):
    t, width = proj.shape
    c3 = conv_w.shape[1]
    assert width == 2 * c3
    c = c3 // 3
    tiles_per_seq = seq // ts
    r8 = ts // SUBLANES
    last8 = t // SUBLANES - 1
    return pl.pallas_call(
        functools.partial(_hyena_pre_kernel, tiles_per_seq),
        grid=(t // ts,),
        in_specs=[
            pl.BlockSpec((ts, c3), lambda i: (i, 1)),
            pl.BlockSpec((SUBLANES, c3), lambda i: (jnp.maximum(i * r8 - 1, 0), 1)),
            pl.BlockSpec((SUBLANES, c3), lambda i: (jnp.minimum((i + 1) * r8, last8), 1)),
            pl.BlockSpec((3, c3), lambda i: (0, 0)),
            pl.BlockSpec((1, c3), lambda i: (0, 0)),
        ],
        out_specs=[pl.BlockSpec((ts, c), lambda i: (i, 0))] * 2,
        out_shape=[jax.ShapeDtypeStruct((t, c), BF16)] * 2,
        compiler_params=_cparams("arbitrary"),
        name="hyena_short_conv",
    )(proj, proj, proj, conv_w, conv_b[None, :])


DFT_FBLK = 512
DFT_SPLIT = 64


def _dft_matrices(seq):
    n = 2 * seq
    k = np.arange(seq)[:, None]
    a = np.arange(seq // DFT_SPLIT)[None, :]
    b = np.arange(DFT_SPLIT)[None, :]
    ang1 = 2.0 * np.pi * ((k * a * DFT_SPLIT) % n) / n
    ang2 = 2.0 * np.pi * ((k * b) % n) / n
    c1, s1 = (jnp.repeat(jnp.asarray(f(ang1), F32), DFT_SPLIT, axis=1) for f in (np.cos, np.sin))
    c2, s2 = (jnp.tile(jnp.asarray(f(ang2), F32), (1, seq // DFT_SPLIT)) for f in (np.cos, np.sin))
    cosm = c1 * c2 - s1 * s2
    nsinm = -(s1 * c2 + c1 * s2)
    alt = jnp.asarray(1.0 - 2.0 * (np.arange(seq) % 2), F32)
    row = lax.broadcasted_iota(I32, (seq, seq), 0)
    col = lax.broadcasted_iota(I32, (seq, seq), 1)
    nb = seq // DFT_FBLK
    f_im = jnp.where(row == 0, alt[None, :], nsinm)
    fwd = jnp.concatenate([cosm.reshape(nb, DFT_FBLK, seq), f_im.reshape(nb, DFT_FBLK, seq)], axis=1)
    fwd = fwd.reshape(n, seq).astype(BF16)
    scale = jnp.where(col == 0, 1.0 / n, 2.0 / n)
    g_re = cosm * scale
    g_im = jnp.where(col == 0, alt[:, None], nsinm) * scale
    inv = jnp.concatenate([g_re.reshape(seq, nb, DFT_FBLK), g_im.reshape(seq, nb, DFT_FBLK)], axis=2)
    return fwd, inv.reshape(seq, n).astype(BF16)


def _dft_filter_kernel(f_ref, h_ref, k_ref, acc_ref):
    kk = pl.program_id(1)

    @pl.when(kk == 0)
    def _():
        acc_ref[...] = jnp.zeros_like(acc_ref)

    def accumulate(rows, cols, v):
        acc_ref[rows, cols] += v

    _dot_tiles(f_ref, h_ref, accumulate)

    @pl.when(kk == pl.num_programs(1) - 1)
    def _():
        c = k_ref.shape[1]
        u = acc_ref[...]
        uf, ub = u[:, :c], u[:, c:]
        row = lax.broadcasted_iota(I32, (DFT_FBLK, c), 0)
        nyq = (row == 0) & (pl.program_id(0) == 0)
        k_ref[:DFT_FBLK, :] = uf[:DFT_FBLK] + ub[:DFT_FBLK]
        k_ref[DFT_FBLK:, :] = jnp.where(nyq, uf[DFT_FBLK:] + ub[DFT_FBLK:], uf[DFT_FBLK:] - ub[DFT_FBLK:])


def _dft_filter(fwd, hcat, tk):
    n, seq = fwd.shape
    c = hcat.shape[1] // 2
    return pl.pallas_call(
        _dft_filter_kernel,
        grid=(n // (2 * DFT_FBLK), seq // tk),
        in_specs=[pl.BlockSpec((2 * DFT_FBLK, tk), lambda i, kk: (i, kk)), pl.BlockSpec((tk, 2 * c), lambda i, kk: (kk, 0))],
        out_specs=pl.BlockSpec((2 * DFT_FBLK, c), lambda i, kk: (i, 0)),
        out_shape=jax.ShapeDtypeStruct((n, c), F32),
        scratch_shapes=[pltpu.VMEM((2 * DFT_FBLK, 2 * c), F32)],
        compiler_params=_cparams("arbitrary", "arbitrary"),
        name="dft_filter",
    )(fwd, hcat)


def _dft_fwd_kernel(f_ref, z_ref, k_ref, p_ref, acc_ref):
    kk = pl.program_id(2)

    @pl.when(kk == 0)
    def _():
        acc_ref[...] = jnp.zeros_like(acc_ref)

    def accumulate(rows, cols, v):
        acc_ref[rows, cols] += v

    _dot_tiles(f_ref, z_ref, accumulate)

    @pl.when(kk == pl.num_programs(2) - 1)
    def _():
        ure, uim = acc_ref[:DFT_FBLK, :], acc_ref[DFT_FBLK:, :]
        kre, kim = k_ref[:DFT_FBLK, :], k_ref[DFT_FBLK:, :]
        row = lax.broadcasted_iota(I32, ure.shape, 0)
        real_pair = (row == 0) & (pl.program_id(1) == 0)
        p_ref[:DFT_FBLK, :] = jnp.where(real_pair, ure * kre, ure * kre - uim * kim).astype(p_ref.dtype)
        p_ref[DFT_FBLK:, :] = jnp.where(real_pair, uim * kim, ure * kim + uim * kre).astype(p_ref.dtype)


def _dft_fwd(fwd, z, kspec, batch, tk):
    n, seq = fwd.shape
    c = z.shape[1]
    nfb = n // (2 * DFT_FBLK)
    nk = seq // tk
    return pl.pallas_call(
        _dft_fwd_kernel,
        grid=(batch, nfb, nk),
        in_specs=[
            pl.BlockSpec((2 * DFT_FBLK, tk), lambda b, i, kk: (i, kk)),
            pl.BlockSpec((tk, c), lambda b, i, kk: (b * nk + kk, 0)),
            pl.BlockSpec((2 * DFT_FBLK, c), lambda b, i, kk: (i, 0)),
        ],
        out_specs=pl.BlockSpec((2 * DFT_FBLK, c), lambda b, i, kk: (b * nfb + i, 0)),
        out_shape=jax.ShapeDtypeStruct((batch * n, c), BF16),
        scratch_shapes=[pltpu.VMEM((2 * DFT_FBLK, c), F32)],
        compiler_params=_cparams("arbitrary", "arbitrary", "arbitrary"),
        name="dft_fwd",
    )(fwd, z, kspec)


def _dft_inv_kernel(g_ref, p_ref, z_ref, x0_ref, bias_ref, o_ref, acc_ref):
    kk = pl.program_id(2)

    @pl.when(kk == 0)
    def _():
        acc_ref[...] = jnp.zeros_like(acc_ref)

    def accumulate(rows, cols, v):
        acc_ref[rows, cols] += v

    _dot_tiles(g_ref, p_ref, accumulate)

    @pl.when(kk == pl.num_programs(2) - 1)
    def _():
        zz = z_ref[...].astype(F32)
        o_ref[...] = (x0_ref[...].astype(F32) * (acc_ref[...] + zz * bias_ref[...])).astype(o_ref.dtype)


def _dft_inv(inv, p, z, x0c, hy_bias, batch, tt, tk):
    seq, n = inv.shape
    c = z.shape[1]
    nt = seq // tt
    nk = n // tk
    return pl.pallas_call(
        _dft_inv_kernel,
        grid=(batch, nt, nk),
        in_specs=[
            pl.BlockSpec((tt, tk), lambda b, i, kk: (i, kk)),
            pl.BlockSpec((tk, c), lambda b, i, kk: (b * nk + kk, 0)),
            pl.BlockSpec((tt, c), lambda b, i, kk: (b * nt + i, 0)),
            pl.BlockSpec((tt, c), lambda b, i, kk: (b * nt + i, 0)),
            pl.BlockSpec((1, c), lambda b, i, kk: (0, 0)),
        ],
        out_specs=pl.BlockSpec((tt, c), lambda b, i, kk: (b * nt + i, 0)),
        out_shape=jax.ShapeDtypeStruct((batch * seq, c), BF16),
        scratch_shapes=[pltpu.VMEM((tt, c), F32)],
        compiler_params=_cparams("arbitrary", "arbitrary", "arbitrary"),
        name="dft_inv",
    )(inv, p, z, x0c, hy_bias[None, :])


POOL_HALO = SUBLANES


def _pool_ln_router_kernel(alpha, n_experts, seq, x_ref, prev_ref, next_ref, pw_ref, ps_ref, g_ref, b_ref, rw_ref,
                           xe_ref, afft_ref, ext_ref, mix_ref):
    tm, d = x_ref.shape
    tiles_per_seq = seq // tm
    i = pl.program_id(0) % tiles_per_seq
    ext_ref[:POOL_HALO, :] = jnp.where(i == 0, 0.0, prev_ref[...])
    ext_ref[POOL_HALO:POOL_HALO + tm, :] = x_ref[...]
    ext_ref[POOL_HALO + tm:, :] = jnp.where(i == tiles_per_seq - 1, 0.0, next_ref[...])
    pos = i * tm + lax.broadcasted_iota(I32, (tm, 1), 0)
    group = d // len(POOL_WINDOWS)
    for gi, win in enumerate(POOL_WINDOWS):
        half = win // 2
        cols = slice(gi * group, (gi + 1) * group)
        wsum = ext_ref[POOL_HALO - half:POOL_HALO - half + tm, cols]
        for j in range(1 - half, half):
            wsum = wsum + ext_ref[POOL_HALO + j:POOL_HALO + j + tm, cols]
        count = (jnp.minimum(pos + half, seq) - jnp.maximum(pos - half, 0)).astype(F32)
        dev = wsum / count - x_ref[:, cols]
        mix_ref[:, cols] = jnp.dot(dev.astype(BF16), pw_ref[gi].astype(BF16), preferred_element_type=F32)
    x1 = _layer_norm_rows(alpha * x_ref[...] + mix_ref[...] * ps_ref[...], g_ref[...], b_ref[...])
    aff = _router_affinity(x1, rw_ref, n_experts)
    xe_ref[:, :d] = x1
    xe_ref[:, d:] = aff
    afft_ref[...] = jnp.transpose(aff)[:n_experts, :]


def _pool_ln_router(x, pool_w, pool_scale, g, b, rw_pad, alpha, n_experts, seq, tm=256):
    t, d = x.shape
    r8 = tm // SUBLANES
    last8 = t // SUBLANES - 1
    const = lambda a: pl.BlockSpec(a.shape, lambda i: (0,) * a.ndim)
    return pl.pallas_call(
        functools.partial(_pool_ln_router_kernel, alpha, n_experts, seq),
        grid=(t // tm,),
        in_specs=[
            pl.BlockSpec((tm, d), lambda i: (i, 0)),
            pl.BlockSpec((SUBLANES, d), lambda i: (jnp.maximum(i * r8 - 1, 0), 0)),
            pl.BlockSpec((SUBLANES, d), lambda i: (jnp.minimum((i + 1) * r8, last8), 0)),
            const(pool_w), const(pool_scale), const(g), const(b), const(rw_pad),
        ],
        out_specs=[pl.BlockSpec((tm, d + LANES), lambda i: (i, 0)), pl.BlockSpec((n_experts, tm), lambda i: (0, i))],
        out_shape=[jax.ShapeDtypeStruct((t, d + LANES), F32), jax.ShapeDtypeStruct((n_experts, t), F32)],
        scratch_shapes=[pltpu.VMEM((tm + 2 * POOL_HALO, d), F32), pltpu.VMEM((tm, d), F32)],
        compiler_params=_cparams("arbitrary"),
        name="pool_ln_router",
    )(x, x, x, pool_w, pool_scale, g, b, rw_pad)


MOE_TILE = 256
MOE_TAB = 32
MOE_JBLK = 128
MOE_WIN = 16


def _lane_cumsum(mask_f, tri):
    rows, n = mask_f.shape
    run = jnp.zeros((rows, 1), F32)
    parts, starts = [], []
    for c in range(n // MOE_TILE):
        starts.append(run)
        m = mask_f[:, c * MOE_TILE:(c + 1) * MOE_TILE].astype(BF16)
        cs = jnp.dot(m, tri, preferred_element_type=F32) + run
        parts.append(cs)
        run = cs[:, MOE_TILE - 1:MOE_TILE]
    starts.append(run)
    return jnp.concatenate(parts, axis=1), starts


def _moe_select_kernel(cap, afft_ref, idx_ref, tok_ref, tab_ref, csel_ref):
    n_exp, seq = afft_ref.shape
    bits = pltpu.bitcast(afft_ref[...], I32)

    def search(it, thr):
        cand = thr | jnp.left_shift(jnp.int32(1), 30 - it)
        cnt = jnp.sum(jnp.where(bits >= cand, 1.0, 0.0), axis=1, keepdims=True)
        return jnp.where(cnt >= cap, cand, thr)

    thr = lax.fori_loop(0, 31, search, jnp.zeros((n_exp, 1), I32))
    gt = bits > thr
    eq = bits == thr
    need = cap - jnp.sum(jnp.where(gt, 1.0, 0.0), axis=1, keepdims=True)
    r = lax.broadcasted_iota(I32, (MOE_TILE, MOE_TILE), 0)
    c = lax.broadcasted_iota(I32, (MOE_TILE, MOE_TILE), 1)
    tri = jnp.where(r <= c, 1.0, 0.0).astype(BF16)
    ceq, _ = _lane_cumsum(jnp.where(eq, 1.0, 0.0), tri)
    sel = jnp.where(gt, 1.0, jnp.where(eq & (ceq <= need), 1.0, 0.0))
    csel, starts = _lane_cumsum(sel, tri)
    csel_ref[...] = csel
    lane = lax.broadcasted_iota(I32, (n_exp, MOE_TAB), 1)
    tab = jnp.zeros((n_exp, MOE_TAB), I32)
    for ti, s in enumerate(starts):
        tab = jnp.where(lane == ti, s.astype(I32), tab)
    tab_ref[...] = tab

    def per_expert(e, carry):
        row = csel_ref[pl.ds(e, 1), :]
        for jb in range(cap // MOE_JBLK):
            jcol = (lax.broadcasted_iota(I32, (MOE_JBLK, LANES), 0) + jb * MOE_JBLK).astype(F32)
            acc = jnp.zeros((MOE_JBLK, LANES), F32)
            for tc in range(seq // LANES):
                acc = acc + jnp.where(row[:, tc * LANES:(tc + 1) * LANES] <= jcol, 1.0, 0.0)
            tokcol = jnp.sum(acc, axis=1, keepdims=True).astype(I32)
            tokb = jnp.broadcast_to(tokcol, (MOE_JBLK, LANES))
            tok_ref[pl.ds(pl.multiple_of(e * cap + jb * MOE_JBLK, MOE_JBLK), MOE_JBLK), :] = tokb
            idx_ref[e, :, jb * MOE_JBLK:(jb + 1) * MOE_JBLK] = jnp.transpose(tokb)[0:1, :]
        return carry

    lax.fori_loop(0, n_exp, per_expert, 0)


def _moe_select(afft, batch, seq, cap):
    n_exp = afft.shape[0]
    assert seq // MOE_TILE + 1 <= MOE_TAB
    return pl.pallas_call(
        functools.partial(_moe_select_kernel, cap),
        grid=(batch,),
        in_specs=[pl.BlockSpec((n_exp, seq), lambda b: (0, b))],
        out_specs=[
            pl.BlockSpec((None, n_exp, 1, cap), lambda b: (b, 0, 0, 0)),
            pl.BlockSpec((n_exp * cap, LANES), lambda b: (b, 0)),
            pl.BlockSpec((n_exp, MOE_TAB), lambda b: (b, 0)),
        ],
        out_shape=[
            jax.ShapeDtypeStruct((batch, n_exp, 1, cap), I32),
            jax.ShapeDtypeStruct((batch * n_exp * cap, LANES), I32),
            jax.ShapeDtypeStruct((batch * n_exp, MOE_TAB), I32),
        ],
        scratch_shapes=[pltpu.VMEM((n_exp, seq), F32)],
        compiler_params=_cparams("arbitrary"),
        name="moe_select",
    )(afft)


MOE_PAIR = 2


MOE_CONV_ROWS = 64


def _moe_expert_kernel(seq, cap, cur0, cur1, nxt0, nxt1, x_hbm, w1_ref, w3_ref, w2_ref, y_ref,
                       xs, xb, gate, acc, w1b, w3b, w2b, sem):
    e, h, f = pl.program_id(0), pl.program_id(1), pl.program_id(2)
    n_exp, n_half, nf = pl.num_programs(0), pl.num_programs(1), pl.num_programs(2)
    rows = MOE_PAIR * cap
    d = xb.shape[1]
    step = e * n_half + h

    def gather(idx_refs, half, lo, hi):
        for s, idx_ref in enumerate(idx_refs):
            a, b = max(lo, s * cap), min(hi, (s + 1) * cap)
            seq_base = (half * MOE_PAIR + s) * seq

            def start_row(r, c):
                src = seq_base + idx_ref[0, r - s * cap]
                pltpu.make_async_copy(x_hbm.at[pl.ds(src, 1)], xs.at[pl.ds(r, 1)], sem.at[0]).start()
                return c

            if a < b:
                lax.fori_loop(a, b, start_row, 0)

    @pl.when((step == 0) & (f == 0))
    def _():
        gather((cur0, cur1), h, 0, rows)

    @pl.when(f == 0)
    def _():
        pltpu.make_async_copy(x_hbm.at[pl.ds(0, rows)], xs, sem.at[0]).wait()
        lane = lax.broadcasted_iota(I32, (MOE_CONV_ROWS, LANES), 1)
        for r in range(0, rows, MOE_CONV_ROWS):
            x = xs[r:r + MOE_CONV_ROWS].reshape(MOE_CONV_ROWS, xs.shape[2])
            xb[r:r + MOE_CONV_ROWS, :] = x[:, :d].astype(BF16)
            gate[r:r + MOE_CONV_ROWS, :] = jnp.sum(jnp.where(lane == e, x[:, d:], 0.0), axis=1, keepdims=True)
        acc[...] = jnp.zeros_like(acc)

    @pl.when(step + 1 < n_exp * n_half)
    def _():
        nxt_h = (step + 1) % n_half
        per = rows // nf
        for fi in range(nf):
            @pl.when(f == fi)
            def _():
                gather((nxt0, nxt1), nxt_h, fi * per, (fi + 1) * per)

    for r in range(0, d, DOT_TM):
        w1b[r:r + DOT_TM, :] = w1_ref[r:r + DOT_TM, :].astype(BF16)
        w3b[r:r + DOT_TM, :] = w3_ref[r:r + DOT_TM, :].astype(BF16)
    for c in range(0, d, DOT_TN):
        w2b[:, c:c + DOT_TN] = w2_ref[:, c:c + DOT_TN].astype(BF16)

    for r in range(0, rows, DOT_TM):
        rs = slice(r, r + DOT_TM)
        xv = xb[rs, :]
        h1 = jnp.dot(xv, w1b[...], preferred_element_type=F32)
        h3 = jnp.dot(xv, w3b[...], preferred_element_type=F32)
        hh = (h1 / (1.0 + jnp.exp(-h1)) * h3).astype(BF16)
        for c in range(0, d, DOT_TN):
            acc[rs, c:c + DOT_TN] += jnp.dot(hh, w2b[:, c:c + DOT_TN], preferred_element_type=F32)

    @pl.when(f == nf - 1)
    def _():
        for r in range(0, rows, DOT_TM):
            y_ref[r:r + DOT_TM, :] = (acc[r:r + DOT_TM, :] * gate[r:r + DOT_TM, :]).astype(y_ref.dtype)


def _moe_experts(x1e, idx, w1, w3, w2, batch, seq, cap, tf):
    n_exp, d, ff = w1.shape
    n_half = batch // MOE_PAIR
    rows = MOE_PAIR * cap
    width = x1e.shape[1]
    x3 = x1e.reshape(x1e.shape[0], 1, width)

    def nxt(e, h):
        s = jnp.minimum(e * n_half + h + 1, n_exp * n_half - 1)
        return s // n_half, s % n_half

    idx_spec = lambda fn: pl.BlockSpec((None, None, 1, cap), fn, memory_space=pltpu.SMEM)
    return pl.pallas_call(
        functools.partial(_moe_expert_kernel, seq, cap),
        grid=(n_exp, n_half, ff // tf),
        in_specs=[
            idx_spec(lambda e, h, f: (h * MOE_PAIR, e, 0, 0)),
            idx_spec(lambda e, h, f: (h * MOE_PAIR + 1, e, 0, 0)),
            idx_spec(lambda e, h, f: (nxt(e, h)[1] * MOE_PAIR, nxt(e, h)[0], 0, 0)),
            idx_spec(lambda e, h, f: (nxt(e, h)[1] * MOE_PAIR + 1, nxt(e, h)[0], 0, 0)),
            pl.BlockSpec(memory_space=pl.ANY),
            pl.BlockSpec((None, d, tf), lambda e, h, f: (e, 0, f)),
            pl.BlockSpec((None, d, tf), lambda e, h, f: (e, 0, f)),
            pl.BlockSpec((None, tf, d), lambda e, h, f: (e, f, 0)),
        ],
        out_specs=pl.BlockSpec((rows, d), lambda e, h, f: (e * n_half + h, 0)),
        out_shape=jax.ShapeDtypeStruct((n_exp * batch * cap, d), BF16),
        scratch_shapes=[
            pltpu.VMEM((rows, 1, width), F32),
            pltpu.VMEM((rows, d), BF16),
            pltpu.VMEM((rows, 1), F32),
            pltpu.VMEM((rows, d), F32),
            pltpu.VMEM((d, tf), BF16),
            pltpu.VMEM((d, tf), BF16),
            pltpu.VMEM((tf, d), BF16),
            pltpu.SemaphoreType.DMA((1,)),
        ],
        compiler_params=_cparams("arbitrary", "arbitrary", "arbitrary"),
        name="moe_experts",
    )(idx, idx, idx, idx, x3, w1, w3, w2)


def _moe_combine_kernel(alpha, n_exp, batch, cap, tab_ref, y_hbm, tok_hbm, x_ref, g_ref, b_ref, o_ref,
                        ybuf, tokbuf, acc, sem):
    b, i = pl.program_id(0), pl.program_id(1)
    tm = x_ref.shape[0]

    @pl.when((b == 0) & (i == 0))
    def _():
        ybuf[...] = jnp.zeros_like(ybuf)
        tokbuf[...] = jnp.zeros_like(tokbuf)

    def window_copies(src_y, src_t, off):
        dst = pl.ds(pl.multiple_of(off * MOE_WIN, MOE_WIN), MOE_WIN)
        return (pltpu.make_async_copy(y_hbm.at[pl.ds(pl.multiple_of(src_y, MOE_WIN), MOE_WIN)], ybuf.at[dst], sem.at[0]),
                pltpu.make_async_copy(tok_hbm.at[pl.ds(pl.multiple_of(src_t, MOE_WIN), MOE_WIN)], tokbuf.at[dst], sem.at[1]))

    def per_expert(e, off):
        base = (b * n_exp + e) * MOE_TAB + i
        s0, s1 = tab_ref[base], tab_ref[base + 1]
        w0 = s0 // MOE_WIN
        nw = jnp.where(s1 > s0, (s1 + MOE_WIN - 1) // MOE_WIN - w0, 0)

        def per_window(w, off2):
            for cp in window_copies((e * batch + b) * cap + (w0 + w) * MOE_WIN,
                                    (b * n_exp + e) * cap + (w0 + w) * MOE_WIN, off2):
                cp.start()
            return off2 + 1

        return lax.fori_loop(0, nw, per_window, off)

    nwin = lax.fori_loop(0, n_exp, per_expert, 0)

    def wait_window(w, c):
        for cp in window_copies(0, 0, w):
            cp.wait()
        return c

    lax.fori_loop(0, nwin, wait_window, 0)

    nrows = nwin * MOE_WIN
    acc[...] = jnp.zeros_like(acc)
    sub = lax.broadcasted_iota(I32, (MOE_TILE, LANES), 0)
    lane = lax.broadcasted_iota(I32, (MOE_TILE, LANES), 1)

    def chunk(c, carry):
        rows = pl.ds(pl.multiple_of(c * MOE_TILE, MOE_TILE), MOE_TILE)
        t_local = tokbuf[rows, :] - i * tm
        valid = (sub + c * MOE_TILE) < nrows
        halves = [jnp.where(valid & (t_local == lane + k * LANES), 1.0, 0.0) for k in range(tm // LANES)]
        onehot = jnp.transpose(jnp.concatenate(halves, axis=1)).astype(BF16)
        for cc in range(0, acc.shape[1], DOT_TN):
            acc[:, cc:cc + DOT_TN] += jnp.dot(onehot, ybuf[rows, cc:cc + DOT_TN], preferred_element_type=F32)
        return carry

    lax.fori_loop(0, (nrows + MOE_TILE - 1) // MOE_TILE, chunk, 0)
    o_ref[...] = _layer_norm_rows(alpha * x_ref[...] + acc[...], g_ref[...], b_ref[...])


def _moe_combine(x1e, y, tok, tab, g, b, alpha, batch, seq, cap, n_exp):
    t = x1e.shape[0]
    d = y.shape[1]
    tm = MOE_TILE
    n_tiles = seq // tm
    max_rows = n_exp * (tm + MOE_WIN)
    max_rows = (max_rows + MOE_TILE - 1) // MOE_TILE * MOE_TILE
    return pl.pallas_call(
        functools.partial(_moe_combine_kernel, alpha, n_exp, batch, cap),
        grid_spec=pltpu.PrefetchScalarGridSpec(
            num_scalar_prefetch=1,
            grid=(batch, n_tiles),
            in_specs=[
                pl.BlockSpec(memory_space=pl.ANY),
                pl.BlockSpec(memory_space=pl.ANY),
                pl.BlockSpec((tm, d), lambda bb, i, tab_r: (bb * n_tiles + i, 0)),
                pl.BlockSpec((1, d), lambda bb, i, tab_r: (0, 0)),
                pl.BlockSpec((1, d), lambda bb, i, tab_r: (0, 0)),
            ],
            out_specs=pl.BlockSpec((tm, d), lambda bb, i, tab_r: (bb * n_tiles + i, 0)),
            scratch_shapes=[
                pltpu.VMEM((max_rows, d), BF16),
                pltpu.VMEM((max_rows, LANES), I32),
                pltpu.VMEM((tm, d), F32),
                pltpu.SemaphoreType.DMA((2,)),
            ],
        ),
        out_shape=jax.ShapeDtypeStruct((t, d), F32),
        compiler_params=_cparams("arbitrary", "arbitrary"),
        name="moe_combine",
    )(tab.reshape(-1), y, tok, x1e, g, b)


def _moe_block(x1e, afft, g, b, w1, w3, w2, alpha, batch, seq):
    n_exp = w1.shape[0]
    cap = EC_CAPACITY_FACTOR * seq // n_exp
    idx, tok, tab = _moe_select(afft, batch, seq, cap)
    y = _moe_experts(x1e, idx, w1, w3, w2, batch, seq, cap, tf=256)
    return _moe_combine(x1e, y, tok, tab, g, b, alpha, batch, seq, cap, n_exp)


def kernel(x, mix_w_in, mix_w_out, hy_conv_w, hy_conv_b, hy_ffn_w1, hy_ffn_b1, hy_ffn_w2, hy_ffn_b2, hy_ffn_w3,
           hy_ffn_b3, hy_ffn_w4, hy_sin_freq, hy_bias, pool_w, pool_scale, ln_mix_g, ln_mix_b, ln_ffn_g, ln_ffn_b,
           router_w, exp_w1, exp_w3, exp_w2):
    batch, seq, d = x.shape
    depth = ln_mix_g.shape[0]
    alpha = (2 * depth) ** 0.25
    n_exp = router_w.shape[2]
    xt = x.reshape(batch * seq, d)
    row = lambda v: v[None, :]
    for layer in range(depth):
        i = layer // 2
        rw_pad = jnp.pad(router_w[layer], ((0, 0), (0, LANES - n_exp)))
        if layer % 2 == 0:
            proj = _matmul(xt, mix_w_in[i], BF16, 512, 1024)
            attn = _dilated_attention(proj, batch, seq)
            hcat = _hyena_filters(seq, hy_ffn_w1[i], hy_ffn_b1[i], hy_ffn_w2[i], hy_ffn_b2[i], hy_ffn_w3[i],
                                  hy_ffn_b3[i], hy_ffn_w4[i], hy_sin_freq[i])
            z, x0c = _hyena_pre(proj, hy_conv_w[i], hy_conv_b[i], seq)
            fwd, inv = _dft_matrices(seq)
            kspec = _dft_filter(fwd, hcat, 2048)
            spec = _dft_fwd(fwd, z, kspec, batch, 2048)
            hyena = _dft_inv(inv, spec, z, x0c, hy_bias[i], batch, 1024, 2048)
            mixed = _matmul(jnp.concatenate([attn, hyena], axis=1), mix_w_out[i], F32, 512, 1024)
            x1e, afft = _ln_router(xt, mixed, row(ln_mix_g[layer]), row(ln_mix_b[layer]), rw_pad, alpha, n_exp)
        else:
            x1e, afft = _pool_ln_router(xt, pool_w[i], row(pool_scale[i]), row(ln_mix_g[layer]), row(ln_mix_b[layer]),
                                        rw_pad, alpha, n_exp, seq)
        xt = _moe_block(x1e, afft, row(ln_ffn_g[layer]), row(ln_ffn_b[layer]), exp_w1[layer], exp_w3[layer],
                        exp_w2[layer], alpha, batch, seq)
    return xt.reshape(batch, seq, d)
```

```python
import functools
import math

import numpy as np
import jax
import jax.numpy as jnp
from jax import lax
from jax.experimental import pallas as pl
from jax.experimental.pallas import tpu as pltpu

F32 = jnp.float32
BF16 = jnp.bfloat16
I32 = jnp.int32

LANES = 128
SUBLANES = 8
VMEM_LIMIT_BYTES = 56 * 1024 * 1024

ATTN_HEADS = 8
HEAD_DIM = 128
DILATED_BRANCHES = ((128, 1), (512, 4), (2048, 16))
POOL_WINDOWS = (2, 4, 8, 16)
N_EXPERTS = 16
EC_CAPACITY_FACTOR = 2
HYENA_POS_DIM = 33
HYENA_DECAY_FAST = 0.3
HYENA_DECAY_SLOW = 1.5
HYENA_DECAY_TARGET = 1e-2
LN_EPS = 1e-5
NEG_BIG = -1e30


def _cparams(*sem):
    return pltpu.CompilerParams(dimension_semantics=sem, vmem_limit_bytes=VMEM_LIMIT_BYTES)


DOT_TM = 256
DOT_TN = 512


def _dot_tiles(a_ref, b_ref, emit, cast=False):
    m, n = a_ref.shape[0], b_ref.shape[1]
    tm, tn = min(DOT_TM, m), min(DOT_TN, n)
    for r in range(0, m, tm):
        a = a_ref[r:r + tm, :]
        a = a.astype(BF16) if cast else a
        for c in range(0, n, tn):
            emit(slice(r, r + tm), slice(c, c + tn), jnp.dot(a, b_ref[:, c:c + tn], preferred_element_type=F32))


def _mm_kernel(n_in, *refs):
    x_refs, w_ref, o_ref, wb_ref = refs[:n_in], refs[n_in], refs[n_in + 1], refs[n_in + 2]

    @pl.when(pl.program_id(1) == 0)
    def _():
        for r in range(0, w_ref.shape[0], DOT_TM):
            wb_ref[r:r + DOT_TM, :] = w_ref[r:r + DOT_TM, :].astype(BF16)

    tm, tn = x_refs[0].shape[0], o_ref.shape[1]
    for r in range(0, tm, DOT_TM):
        xs = [x_ref[r:r + DOT_TM, :].astype(BF16) for x_ref in x_refs]
        for c in range(0, tn, DOT_TN):
            k0, v = 0, None
            for xv in xs:
                part = jnp.dot(xv, wb_ref[k0:k0 + xv.shape[1], c:c + DOT_TN], preferred_element_type=F32)
                v = part if v is None else v + part
                k0 += xv.shape[1]
            o_ref[r:r + DOT_TM, c:c + DOT_TN] = v.astype(o_ref.dtype)


def _matmul(xs, w, out_dtype, tm, tn):
    m = xs[0].shape[0]
    k, n = w.shape
    assert sum(x.shape[1] for x in xs) == k
    return pl.pallas_call(
        functools.partial(_mm_kernel, len(xs)),
        grid=(n // tn, m // tm),
        in_specs=[pl.BlockSpec((tm, x.shape[1]), lambda j, i: (i, 0)) for x in xs]
        + [pl.BlockSpec((k, tn), lambda j, i: (0, j))],
        out_specs=pl.BlockSpec((tm, tn), lambda j, i: (i, j)),
        out_shape=jax.ShapeDtypeStruct((m, n), out_dtype),
        scratch_shapes=[pltpu.VMEM((k, tn), BF16)],
        compiler_params=_cparams("arbitrary", "arbitrary"),
        name="matmul",
    )(*xs, w)


def _layer_norm_rows(v, g, b):
    mu = jnp.mean(v, axis=-1, keepdims=True)
    c = v - mu
    var = jnp.mean(c * c, axis=-1, keepdims=True)
    return c * lax.rsqrt(var + LN_EPS) * g + b


def _bf16_pieces(v):
    hi = v.astype(BF16)
    return hi, (v - hi.astype(F32)).astype(BF16)


def _router_affinity(x1, rw_ref, n_experts):
    xh, xl = _bf16_pieces(x1)
    wh, wl = _bf16_pieces(rw_ref[...])
    dot = functools.partial(jnp.dot, preferred_element_type=F32)
    logits = dot(xh, wh) + (dot(xh, wl) + dot(xl, wh))
    lane = lax.broadcasted_iota(I32, logits.shape, 1)
    logits = jnp.where(lane < n_experts, logits, NEG_BIG)
    mx = jnp.max(logits, axis=-1, keepdims=True)
    e = jnp.exp(logits - mx)
    return e / jnp.sum(e, axis=-1, keepdims=True)


def _ln_router_kernel(alpha, n_experts, x_ref, m_ref, g_ref, b_ref, rw_ref, xe_ref, afft_ref):
    d = x_ref.shape[1]
    x1 = _layer_norm_rows(alpha * x_ref[...] + m_ref[...].astype(F32), g_ref[...], b_ref[...])
    aff = _router_affinity(x1, rw_ref, n_experts)
    xe_ref[:, :d] = x1
    xe_ref[:, d:] = aff
    afft_ref[...] = jnp.transpose(aff)[:n_experts, :]


def _ln_router(x, mixed, g, b, rw_pad, alpha, n_experts, tm=256):
    t, d = x.shape
    return pl.pallas_call(
        functools.partial(_ln_router_kernel, alpha, n_experts),
        grid=(t // tm,),
        in_specs=[
            pl.BlockSpec((tm, d), lambda i: (i, 0)),
            pl.BlockSpec((tm, d), lambda i: (i, 0)),
            pl.BlockSpec((1, d), lambda i: (0, 0)),
            pl.BlockSpec((1, d), lambda i: (0, 0)),
            pl.BlockSpec((d, LANES), lambda i: (0, 0)),
        ],
        out_specs=[pl.BlockSpec((tm, d + LANES), lambda i: (i, 0)), pl.BlockSpec((n_experts, tm), lambda i: (0, i))],
        out_shape=[jax.ShapeDtypeStruct((t, d + LANES), F32), jax.ShapeDtypeStruct((n_experts, t), F32)],
        compiler_params=_cparams("arbitrary"),
        name="ln_router",
    )(x, mixed, g, b, rw_pad)


ATTN_QBLK = 128
ATTN_UNROLL = 4
ATTN_HALF = 64
ATTN_WINDOW_LEADS = (0, ATTN_HALF, 2 * ATTN_HALF)


def _attn_branch(slope, dil, lc, q_cm, k_cm, v_cm, o_cm, l_cm, seq, bias_ref):
    win = min(ATTN_QBLK + 2 * ATTN_HALF, lc)
    blocks_per_class = lc // ATTN_QBLK
    scale = 1.0 / math.sqrt(HEAD_DIM)
    diff = lax.broadcasted_iota(I32, (ATTN_QBLK, win), 1) - lax.broadcasted_iota(I32, (ATTN_QBLK, win), 0)
    for case in range(len(ATTN_WINDOW_LEADS)):
        rel = jnp.abs(diff - ATTN_WINDOW_LEADS[case])
        bias_ref[case, :, :win] = jnp.where(rel <= ATTN_HALF, -(slope * dil) * rel.astype(F32), NEG_BIG)

    def body(blk, carry):
        cls = blk // blocks_per_class
        q0 = (blk % blocks_per_class) * ATTN_QBLK
        w0 = jnp.clip(q0 - ATTN_HALF, 0, lc - win)
        qs = pl.multiple_of(cls * lc + q0, ATTN_HALF)
        ks = pl.multiple_of(cls * lc + w0, ATTN_HALF)
        q = q_cm[pl.ds(qs, ATTN_QBLK), :]
        k = k_cm[pl.ds(ks, win), :]
        v = v_cm[pl.ds(ks, win), :]
        s = lax.dot_general(q, k, (((1,), (1,)), ((), ())), preferred_element_type=F32) * scale
        bias = bias_ref[(q0 - w0) // ATTN_HALF, :, :win]
        s = jnp.where(bias > 0.5 * NEG_BIG, s + bias, NEG_BIG)
        m = jnp.max(s, axis=-1, keepdims=True)
        p = jnp.exp(s - m)
        l = jnp.sum(p, axis=-1, keepdims=True)
        o = jnp.dot(p.astype(BF16), v, preferred_element_type=F32) / l
        o_cm[pl.ds(qs, ATTN_QBLK), :] = o
        l_cm[pl.ds(qs, ATTN_QBLK), :] = jnp.broadcast_to(m + jnp.log(l), (ATTN_QBLK, HEAD_DIM))
        return carry

    lax.fori_loop(0, seq // ATTN_QBLK, body, 0, unroll=ATTN_UNROLL)


def _attn_kernel(slopes_ref, q_ref, k_ref, v_ref, out_ref, qf, kf, vf, qc, kc, vc, ocm, lcm, onat, lnat, bias_ref):
    seq = q_ref.shape[0]
    slope = slopes_ref[pl.program_id(1)]
    qf[...] = q_ref[...].astype(F32)
    kf[...] = k_ref[...].astype(F32)
    vf[...] = v_ref[...].astype(F32)
    for bi, (window, dil) in enumerate(DILATED_BRANCHES):
        assert window // (2 * dil) == ATTN_HALF
        lc = seq // dil
        if dil == 1:
            _attn_branch(slope, dil, lc, q_ref, k_ref, v_ref, onat.at[bi], lnat.at[bi], seq, bias_ref)
            continue
        for r in range(dil):
            qc[r * lc:(r + 1) * lc, :] = qf[pl.ds(r, lc, stride=dil), :].astype(BF16)
            kc[r * lc:(r + 1) * lc, :] = kf[pl.ds(r, lc, stride=dil), :].astype(BF16)
            vc[r * lc:(r + 1) * lc, :] = vf[pl.ds(r, lc, stride=dil), :].astype(BF16)
        _attn_branch(slope, dil, lc, qc, kc, vc, ocm, lcm, seq, bias_ref)
        for r in range(dil):
            onat[bi, pl.ds(r, lc, stride=dil), :] = ocm[r * lc:(r + 1) * lc, :]
            lnat[bi, pl.ds(r, lc, stride=dil), :] = lcm[r * lc:(r + 1) * lc, :]
    lses = [lnat[bi] for bi in range(len(DILATED_BRANCHES))]
    mx = functools.reduce(jnp.maximum, lses)
    ws = [jnp.exp(l - mx) for l in lses]
    num = sum(w * onat[bi] for bi, w in enumerate(ws))
    out_ref[...] = (num / sum(ws)).astype(out_ref.dtype)


def _dilated_attention(proj, batch, seq):
    t = proj.shape[0]
    slopes = jnp.asarray(2.0 ** (-(8.0 / ATTN_HEADS) * np.arange(1, ATTN_HEADS + 1)), F32)
    blk = (seq, HEAD_DIM)
    nb = len(DILATED_BRANCHES)
    return pl.pallas_call(
        _attn_kernel,
        grid_spec=pltpu.PrefetchScalarGridSpec(
            num_scalar_prefetch=1,
            grid=(batch, ATTN_HEADS),
            in_specs=[
                pl.BlockSpec(blk, lambda b, h, s: (b, h)),
                pl.BlockSpec(blk, lambda b, h, s: (b, ATTN_HEADS + h)),
                pl.BlockSpec(blk, lambda b, h, s: (b, 2 * ATTN_HEADS + h)),
            ],
            out_specs=pl.BlockSpec(blk, lambda b, h, s: (b, h)),
            scratch_shapes=[pltpu.VMEM(blk, F32)] * 3 + [pltpu.VMEM(blk, BF16)] * 3 + [pltpu.VMEM(blk, F32)] * 2
            + [pltpu.VMEM((nb,) + blk, F32)] * 2
            + [pltpu.VMEM((len(ATTN_WINDOW_LEADS), ATTN_QBLK, ATTN_QBLK + 2 * ATTN_HALF), F32)],
        ),
        out_shape=jax.ShapeDtypeStruct((t, ATTN_HEADS * HEAD_DIM), BF16),
        compiler_params=_cparams("arbitrary", "arbitrary"),
        name="dilated_attention",
    )(slopes, proj, proj, proj)


HYENA_ORDER_PAD = LANES


def _filter_kernel(z_ref, w1, b1, w2, b2, w3, b3, w4, fr, absdelta_ref, h_ref):
    hi = lax.Precision.HIGHEST
    f = fr[...]
    z = z_ref[...]
    h = jnp.sin(f * (jnp.dot(z, w1[...], preferred_element_type=F32, precision=hi) + b1[...]))
    h = jnp.sin(f * (jnp.dot(h, w2[...], preferred_element_type=F32, precision=hi) + b2[...]))
    h = jnp.sin(f * (jnp.dot(h, w3[...], preferred_element_type=F32, precision=hi) + b3[...]))
    h = jnp.dot(h, w4[...], preferred_element_type=F32, precision=hi)
    t = z[:, 0:1]
    h = h * jnp.exp(-t * absdelta_ref[...])
    c = h.shape[1] // 2
    row = lax.broadcasted_iota(I32, h.shape, 0) + pl.program_id(0) * h.shape[0]
    col = lax.broadcasted_iota(I32, h.shape, 1)
    h_ref[...] = jnp.where((row == 0) & (col >= c), 0.0, h).astype(h_ref.dtype)


def _position_features(seq):
    t = np.linspace(0.0, 1.0, seq)[:, None]
    bands = (HYENA_POS_DIM - 1) // 2
    w_ang = 2.0 * np.pi * np.arange(seq)[:, None] / seq
    f = np.linspace(1e-4, bands - 1, bands)[None, :]
    z = np.concatenate([t, np.cos(f * w_ang), -np.sin(f * w_ang)], axis=-1)
    return np.pad(z, ((0, 0), (0, HYENA_ORDER_PAD - z.shape[1]))).astype(np.float32)


def _pad2(a, rows, cols):
    return jnp.pad(a, ((0, rows - a.shape[0]), (0, cols - a.shape[1])))


def _hyena_filters(seq, w1, b1, w2, b2, w3, b3, w4, freq, tl=512):
    p = HYENA_ORDER_PAD
    c2 = w4.shape[1]
    c = c2 // 2
    max_decay = math.log(HYENA_DECAY_TARGET) / HYENA_DECAY_FAST
    min_decay = math.log(HYENA_DECAY_TARGET) / HYENA_DECAY_SLOW
    absdelta = np.abs(np.linspace(min_decay, max_decay, c))
    absdelta = jnp.asarray(np.concatenate([absdelta, absdelta])[None, :], F32)
    args = (
        jnp.asarray(_position_features(seq)),
        _pad2(w1, p, p), _pad2(b1[None, :], 1, p), _pad2(w2, p, p), _pad2(b2[None, :], 1, p),
        _pad2(w3, p, p), _pad2(b3[None, :], 1, p), _pad2(w4, p, c2), _pad2(freq[None, :], 1, p), absdelta,
    )
    full = lambda a: pl.BlockSpec(a.shape, lambda i: (0, 0))
    return pl.pallas_call(
        _filter_kernel,
        grid=(seq // tl,),
        in_specs=[pl.BlockSpec((tl, p), lambda i: (i, 0))] + [full(a) for a in args[1:]],
        out_specs=pl.BlockSpec((tl, c2), lambda i: (i, 0)),
        out_shape=jax.ShapeDtypeStruct((seq, c2), BF16),
        compiler_params=_cparams("arbitrary"),
        name="hyena_filters",
    )(*args)


def _shift_rows(u, edge_row, down):
    n = u.shape[0]
    row = lax.broadcasted_iota(I32, u.shape, 0)
    if down:
        return jnp.where(row == 0, edge_row, pltpu.roll(u, 1, 0))
    return jnp.where(row == n - 1, edge_row, pltpu.roll(u, n - 1, 0))


def _hyena_pre_kernel(tiles_per_seq, cur_ref, prev_ref, next_ref, w_ref, b_ref, z_ref, x0_ref):
    i = pl.program_id(0) % tiles_per_seq
    u = cur_ref[...].astype(F32)
    prev_row = jnp.where(i == 0, 0.0, prev_ref[SUBLANES - 1:SUBLANES, :].astype(F32))
    next_row = jnp.where(i == tiles_per_seq - 1, 0.0, next_ref[0:1, :].astype(F32))
    y = (w_ref[0:1, :] * _shift_rows(u, prev_row, True) + w_ref[1:2, :] * u
         + w_ref[2:3, :] * _shift_rows(u, next_row, False) + b_ref[...])
    c = y.shape[1] // 3
    x0_ref[...] = y[:, :c].astype(x0_ref.dtype)
    z_ref[...] = (y[:, 2 * c:] * y[:, c:2 * c]).astype(z_ref.dtype)


def _hyena_pre(proj, conv_w, conv_b, seq, ts=256):
    t, width = proj.shape
    c3 = conv_w.shape[1]
    assert width == 2 * c3
    c = c3 // 3
    tiles_per_seq = seq // ts
    r8 = ts // SUBLANES
    last8 = t // SUBLANES - 1
    return pl.pallas_call(
        functools.partial(_hyena_pre_kernel, tiles_per_seq),
        grid=(t // ts,),
        in_specs=[
            pl.BlockSpec((ts, c3), lambda i: (i, 1)),
            pl.BlockSpec((SUBLANES, c3), lambda i: (jnp.maximum(i * r8 - 1, 0), 1)),
            pl.BlockSpec((SUBLANES, c3), lambda i: (jnp.minimum((i + 1) * r8, last8), 1)),
            pl.BlockSpec((3, c3), lambda i: (0, 0)),
            pl.BlockSpec((1, c3), lambda i: (0, 0)),
        ],
        out_specs=[pl.BlockSpec((ts, c), lambda i: (i, 0))] * 2,
        out_shape=[jax.ShapeDtypeStruct((t, c), BF16)] * 2,
        compiler_params=_cparams("arbitrary"),
        name="hyena_short_conv",
    )(proj, proj, proj, conv_w, conv_b[None, :])


DFT_FBLK = 512
DFT_SPLIT = 64
DFT_GEN_ROWS = 256
DFT_EPI_ROWS = 64
BF16_ROWS = 2 * SUBLANES


def _dft_tables(seq):
    n = 2 * seq
    k = np.arange(seq)[:, None]
    a = np.arange(seq // DFT_SPLIT)[None, :]
    b = np.arange(DFT_SPLIT)[None, :]
    ang1 = 2.0 * np.pi * ((k * a * DFT_SPLIT) % n) / n
    ang2 = 2.0 * np.pi * ((k * b) % n) / n
    return tuple(jnp.asarray(f(a_), F32) for a_ in (ang1, ang2) for f in (np.cos, np.sin))


def _dft_gen_kernel(c1_ref, s1_ref, c2_ref, s2_ref, cos_ref, nsin_ref):
    rows, na = c1_ref.shape
    seq = cos_ref.shape[1]
    reps = DOT_TN // DFT_SPLIT
    c2 = jnp.concatenate([c2_ref[...]] * reps, axis=1)
    s2 = jnp.concatenate([s2_ref[...]] * reps, axis=1)

    c1p, s1p = _bf16_pieces(c1_ref[...]), _bf16_pieces(s1_ref[...])
    for c in range(0, seq, DOT_TN):
        n_of_lane = lax.broadcasted_iota(I32, (na, DOT_TN), 1) + c
        expand = jnp.where(n_of_lane // DFT_SPLIT == lax.broadcasted_iota(I32, (na, DOT_TN), 0), 1.0, 0.0).astype(BF16)
        a, b = (jnp.dot(hi, expand, preferred_element_type=F32) + jnp.dot(lo, expand, preferred_element_type=F32)
                for hi, lo in (c1p, s1p))
        cos_ref[:, c:c + DOT_TN] = (a * c2 - b * s2).astype(cos_ref.dtype)
        nsin_ref[:, c:c + DOT_TN] = (-(b * c2 + a * s2)).astype(nsin_ref.dtype)


def _dft_matrices(seq):
    tables = _dft_tables(seq)
    tr = DFT_GEN_ROWS
    return pl.pallas_call(
        _dft_gen_kernel,
        grid=(seq // tr,),
        in_specs=[pl.BlockSpec((tr, t.shape[1]), lambda i: (i, 0)) for t in tables],
        out_specs=[pl.BlockSpec((tr, seq), lambda i: (i, 0))] * 2,
        out_shape=[jax.ShapeDtypeStruct((seq, seq), BF16)] * 2,
        compiler_params=_cparams("arbitrary"),
        name="dft_matrices",
    )(*tables)


def _nyquist_rows(tk):
    lane = lax.broadcasted_iota(I32, (SUBLANES, tk), 1)
    return jnp.where(lane % 2 == 0, 1.0, -1.0).astype(BF16)


def _dft_accumulate(cos_ref, nsin_ref, u_ref, re_ref, im_ref, ny_ref, kk, first_block):
    @pl.when(kk == 0)
    def _():
        re_ref[...] = jnp.zeros_like(re_ref)
        im_ref[...] = jnp.zeros_like(im_ref)
        ny_ref[...] = jnp.zeros_like(ny_ref)

    def add_re(rows, cols, v):
        re_ref[rows, cols] += v

    def add_im(rows, cols, v):
        im_ref[rows, cols] += v

    _dot_tiles(cos_ref, u_ref, add_re)
    _dot_tiles(nsin_ref, u_ref, add_im)

    @pl.when(first_block)
    def _():
        ny_ref[...] += jnp.dot(_nyquist_rows(u_ref.shape[0]), u_ref[...], preferred_element_type=F32)


def _dft_filter_kernel(cos_ref, nsin_ref, h_ref, kre_ref, kim_ref, re_ref, im_ref, ny_ref):
    i, kk = pl.program_id(0), pl.program_id(1)
    _dft_accumulate(cos_ref, nsin_ref, h_ref, re_ref, im_ref, ny_ref, kk, i == 0)

    @pl.when(kk == pl.num_programs(1) - 1)
    def _():
        c = kre_ref.shape[1]
        kre_ref[...] = re_ref[:, :c] + re_ref[:, c:]
        kim_ref[...] = im_ref[:, :c] - im_ref[:, c:]

        @pl.when(i == 0)
        def _():
            kim_ref[0:1, :] = ny_ref[0:1, :c] + ny_ref[0:1, c:]


def _dft_filter(cosm, nsinm, hcat, tk):
    seq = cosm.shape[0]
    c2 = hcat.shape[1]
    mat = pl.BlockSpec((DFT_FBLK, tk), lambda i, kk: (i, kk))
    return pl.pallas_call(
        _dft_filter_kernel,
        grid=(seq // DFT_FBLK, seq // tk),
        in_specs=[mat, mat, pl.BlockSpec((tk, c2), lambda i, kk: (kk, 0))],
        out_specs=[pl.BlockSpec((DFT_FBLK, c2 // 2), lambda i, kk: (i, 0))] * 2,
        out_shape=[jax.ShapeDtypeStruct((seq, c2 // 2), F32)] * 2,
        scratch_shapes=[pltpu.VMEM((DFT_FBLK, c2), F32)] * 2 + [pltpu.VMEM((SUBLANES, c2), F32)],
        compiler_params=_cparams("arbitrary", "arbitrary"),
        name="dft_filter",
    )(cosm, nsinm, hcat)


def _dft_fwd_kernel(seq, cos_ref, nsin_ref, z_ref, kre_ref, kim_ref, pre_ref, pim_ref, re_ref, im_ref, ny_ref):
    i, kk = pl.program_id(1), pl.program_id(2)
    _dft_accumulate(cos_ref, nsin_ref, z_ref, re_ref, im_ref, ny_ref, kk, i == 0)

    @pl.when(kk == pl.num_programs(2) - 1)
    def _():
        inv_n = 1.0 / (2 * seq)
        for r in range(0, DFT_FBLK, DFT_EPI_ROWS):
            rs = slice(r, r + DFT_EPI_ROWS)
            ure, uim, kre, kim = re_ref[rs, :], im_ref[rs, :], kre_ref[rs, :], kim_ref[rs, :]
            pre_ref[rs, :] = ((ure * kre - uim * kim) * (2 * inv_n)).astype(pre_ref.dtype)
            pim_ref[rs, :] = ((ure * kim + uim * kre) * (2 * inv_n)).astype(pim_ref.dtype)

        @pl.when(i == 0)
        def _():
            head = slice(0, BF16_ROWS)
            ure, uim, kre, kim = re_ref[head, :], im_ref[head, :], kre_ref[head, :], kim_ref[head, :]
            first = lax.broadcasted_iota(I32, ure.shape, 0) == 0
            p_re = jnp.where(first, ure * kre * inv_n, (ure * kre - uim * kim) * (2 * inv_n))
            p_im = jnp.where(first, ny_ref[0:1, :] * kim * inv_n, (ure * kim + uim * kre) * (2 * inv_n))
            pre_ref[head, :] = p_re.astype(pre_ref.dtype)
            pim_ref[head, :] = p_im.astype(pim_ref.dtype)


def _dft_fwd(cosm, nsinm, z, kre, kim, batch, tk):
    seq = cosm.shape[0]
    c = z.shape[1]
    nfb = seq // DFT_FBLK
    nk = seq // tk
    mat = pl.BlockSpec((DFT_FBLK, tk), lambda b, i, kk: (i, kk))
    spec = pl.BlockSpec((DFT_FBLK, c), lambda b, i, kk: (i, 0))
    return pl.pallas_call(
        functools.partial(_dft_fwd_kernel, seq),
        grid=(batch, nfb, nk),
        in_specs=[mat, mat, pl.BlockSpec((tk, c), lambda b, i, kk: (b * nk + kk, 0)), spec, spec],
        out_specs=[pl.BlockSpec((DFT_FBLK, c), lambda b, i, kk: (b * nfb + i, 0))] * 2,
        out_shape=[jax.ShapeDtypeStruct((batch * seq, c), BF16)] * 2,
        scratch_shapes=[pltpu.VMEM((DFT_FBLK, c), F32)] * 2 + [pltpu.VMEM((SUBLANES, c), F32)],
        compiler_params=_cparams("arbitrary", "arbitrary", "arbitrary"),
        name="dft_fwd",
    )(cosm, nsinm, z, kre, kim)


def _dft_inv_kernel(cos_ref, nsin_ref, pre_ref, pim_ref, pny_ref, z_ref, x0_ref, bias_ref, o_ref, acc_ref):
    it, kk = pl.program_id(1), pl.program_id(2)
    tt = acc_ref.shape[0]

    @pl.when(kk == 0)
    def _():
        acc_ref[...] = jnp.zeros_like(acc_ref)

    def accumulate(rows, cols, v):
        acc_ref[rows, cols] += v

    _dot_tiles(cos_ref, pre_ref, accumulate)
    _dot_tiles(nsin_ref, pim_ref, accumulate)

    @pl.when(kk == pl.num_programs(2) - 1)
    def _():
        nyq = pny_ref[0:1, :].astype(F32)
        for r in range(0, tt, DOT_TM):
            rs = slice(r, r + DOT_TM)
            t = it * tt + r + lax.broadcasted_iota(I32, (DOT_TM, 1), 0)
            y = acc_ref[rs, :] + jnp.where(t % 2 == 0, 1.0, -1.0) * nyq
            zz = z_ref[rs, :].astype(F32)
            o_ref[rs, :] = (x0_ref[rs, :].astype(F32) * (y + zz * bias_ref[...])).astype(o_ref.dtype)


def _dft_inv(cosm, nsinm, p_re, p_im, z, x0c, hy_bias, batch, tt, tk):
    seq = cosm.shape[0]
    c = z.shape[1]
    nt = seq // tt
    nk = seq // tk
    mat = pl.BlockSpec((tt, tk), lambda b, i, kk: (i, kk))
    spec = pl.BlockSpec((tk, c), lambda b, i, kk: (b * nk + kk, 0))
    tile = pl.BlockSpec((tt, c), lambda b, i, kk: (b * nt + i, 0))
    return pl.pallas_call(
        _dft_inv_kernel,
        grid=(batch, nt, nk),
        in_specs=[mat, mat, spec, spec, pl.BlockSpec((BF16_ROWS, c), lambda b, i, kk: (b * (seq // BF16_ROWS), 0)),
                  tile, tile, pl.BlockSpec((1, c), lambda b, i, kk: (0, 0))],
        out_specs=tile,
        out_shape=jax.ShapeDtypeStruct((batch * seq, c), BF16),
        scratch_shapes=[pltpu.VMEM((tt, c), F32)],
        compiler_params=_cparams("arbitrary", "arbitrary", "arbitrary"),
        name="dft_inv",
    )(cosm, nsinm, p_re, p_im, p_im, z, x0c, hy_bias[None, :])


POOL_HALO = SUBLANES


def _pool_ln_router_kernel(alpha, n_experts, seq, x_ref, prev_ref, next_ref, pw_ref, ps_ref, g_ref, b_ref, rw_ref,
                           xe_ref, afft_ref, ext_ref, mix_ref):
    tm, d = x_ref.shape
    tiles_per_seq = seq // tm
    i = pl.program_id(0) % tiles_per_seq
    ext_ref[:POOL_HALO, :] = jnp.where(i == 0, 0.0, prev_ref[...])
    ext_ref[POOL_HALO:POOL_HALO + tm, :] = x_ref[...]
    ext_ref[POOL_HALO + tm:, :] = jnp.where(i == tiles_per_seq - 1, 0.0, next_ref[...])
    pos = i * tm + lax.broadcasted_iota(I32, (tm, 1), 0)
    group = d // len(POOL_WINDOWS)
    for gi, win in enumerate(POOL_WINDOWS):
        half = win // 2
        cols = slice(gi * group, (gi + 1) * group)
        wsum = ext_ref[POOL_HALO - half:POOL_HALO - half + tm, cols]
        for j in range(1 - half, half):
            wsum = wsum + ext_ref[POOL_HALO + j:POOL_HALO + j + tm, cols]
        count = (jnp.minimum(pos + half, seq) - jnp.maximum(pos - half, 0)).astype(F32)
        dev = wsum / count - x_ref[:, cols]
        mix_ref[:, cols] = jnp.dot(dev.astype(BF16), pw_ref[gi].astype(BF16), preferred_element_type=F32)
    x1 = _layer_norm_rows(alpha * x_ref[...] + mix_ref[...] * ps_ref[...], g_ref[...], b_ref[...])
    aff = _router_affinity(x1, rw_ref, n_experts)
    xe_ref[:, :d] = x1
    xe_ref[:, d:] = aff
    afft_ref[...] = jnp.transpose(aff)[:n_experts, :]


def _pool_ln_router(x, pool_w, pool_scale, g, b, rw_pad, alpha, n_experts, seq, tm=256):
    t, d = x.shape
    r8 = tm // SUBLANES
    last8 = t // SUBLANES - 1
    const = lambda a: pl.BlockSpec(a.shape, lambda i: (0,) * a.ndim)
    return pl.pallas_call(
        functools.partial(_pool_ln_router_kernel, alpha, n_experts, seq),
        grid=(t // tm,),
        in_specs=[
            pl.BlockSpec((tm, d), lambda i: (i, 0)),
            pl.BlockSpec((SUBLANES, d), lambda i: (jnp.maximum(i * r8 - 1, 0), 0)),
            pl.BlockSpec((SUBLANES, d), lambda i: (jnp.minimum((i + 1) * r8, last8), 0)),
            const(pool_w), const(pool_scale), const(g), const(b), const(rw_pad),
        ],
        out_specs=[pl.BlockSpec((tm, d + LANES), lambda i: (i, 0)), pl.BlockSpec((n_experts, tm), lambda i: (0, i))],
        out_shape=[jax.ShapeDtypeStruct((t, d + LANES), F32), jax.ShapeDtypeStruct((n_experts, t), F32)],
        scratch_shapes=[pltpu.VMEM((tm + 2 * POOL_HALO, d), F32), pltpu.VMEM((tm, d), F32)],
        compiler_params=_cparams("arbitrary"),
        name="pool_ln_router",
    )(x, x, x, pool_w, pool_scale, g, b, rw_pad)


MOE_TILE = 256
MOE_TAB = 32
MOE_JBLK = 128
MOE_WIN = 16


def _lane_cumsum(mask_f, tri):
    rows, n = mask_f.shape
    run = jnp.zeros((rows, 1), F32)
    parts, starts = [], []
    for c in range(n // MOE_TILE):
        starts.append(run)
        m = mask_f[:, c * MOE_TILE:(c + 1) * MOE_TILE].astype(BF16)
        cs = jnp.dot(m, tri, preferred_element_type=F32) + run
        parts.append(cs)
        run = cs[:, MOE_TILE - 1:MOE_TILE]
    starts.append(run)
    return jnp.concatenate(parts, axis=1), starts


def _moe_select_kernel(cap, afft_ref, idx_ref, tok_ref, tab_ref, csel_ref):
    n_exp, seq = afft_ref.shape
    aff = afft_ref[...]

    def search(it, thr_bits):
        cand = thr_bits | jnp.left_shift(jnp.int32(1), 30 - it)
        cnt = jnp.sum(jnp.where(aff >= pltpu.bitcast(cand, F32), 1.0, 0.0), axis=1, keepdims=True)
        return jnp.where(cnt >= cap, cand, thr_bits)

    thr = pltpu.bitcast(lax.fori_loop(0, 31, search, jnp.zeros((n_exp, 1), I32)), F32)
    gt = aff > thr
    eq = aff == thr
    need = cap - jnp.sum(jnp.where(gt, 1.0, 0.0), axis=1, keepdims=True)
    r = lax.broadcasted_iota(I32, (MOE_TILE, MOE_TILE), 0)
    c = lax.broadcasted_iota(I32, (MOE_TILE, MOE_TILE), 1)
    tri = jnp.where(r <= c, 1.0, 0.0).astype(BF16)
    ceq, _ = _lane_cumsum(jnp.where(eq, 1.0, 0.0), tri)
    sel = jnp.where(gt, 1.0, jnp.where(eq & (ceq <= need), 1.0, 0.0))
    csel, starts = _lane_cumsum(sel, tri)
    csel_ref[...] = csel
    lane = lax.broadcasted_iota(I32, (n_exp, MOE_TAB), 1)
    tab = jnp.zeros((n_exp, MOE_TAB), I32)
    for ti, s in enumerate(starts):
        tab = jnp.where(lane == ti, s.astype(I32), tab)
    tab_ref[...] = tab

    def per_expert(e, carry):
        row = csel_ref[pl.ds(e, 1), :]
        for jb in range(cap // MOE_JBLK):
            jcol = (lax.broadcasted_iota(I32, (MOE_JBLK, LANES), 0) + jb * MOE_JBLK).astype(F32)
            acc = jnp.zeros((MOE_JBLK, LANES), F32)
            for tc in range(seq // LANES):
                acc = acc + jnp.where(row[:, tc * LANES:(tc + 1) * LANES] <= jcol, 1.0, 0.0)
            tokcol = jnp.sum(acc, axis=1, keepdims=True).astype(I32)
            tokb = jnp.broadcast_to(tokcol, (MOE_JBLK, LANES))
            tok_ref[pl.ds(pl.multiple_of(e * cap + jb * MOE_JBLK, MOE_JBLK), MOE_JBLK), :] = tokb
            idx_ref[e, :, jb * MOE_JBLK:(jb + 1) * MOE_JBLK] = jnp.transpose(tokb)[0:1, :]
        return carry

    lax.fori_loop(0, n_exp, per_expert, 0)


def _moe_select(afft, batch, seq, cap):
    n_exp = afft.shape[0]
    assert seq // MOE_TILE + 1 <= MOE_TAB
    return pl.pallas_call(
        functools.partial(_moe_select_kernel, cap),
        grid=(batch,),
        in_specs=[pl.BlockSpec((n_exp, seq), lambda b: (0, b))],
        out_specs=[
            pl.BlockSpec((n_exp, None, 1, cap), lambda b: (0, b, 0, 0)),
            pl.BlockSpec((n_exp * cap, LANES), lambda b: (b, 0)),
            pl.BlockSpec((n_exp, MOE_TAB), lambda b: (b, 0)),
        ],
        out_shape=[
            jax.ShapeDtypeStruct((n_exp, batch, 1, cap), I32),
            jax.ShapeDtypeStruct((batch * n_exp * cap, LANES), I32),
            jax.ShapeDtypeStruct((batch * n_exp, MOE_TAB), I32),
        ],
        scratch_shapes=[pltpu.VMEM((n_exp, seq), F32)],
        compiler_params=_cparams("arbitrary"),
        name="moe_select",
    )(afft)


MOE_PAIR = 2


MOE_CONV_ROWS = 64


def _moe_expert_kernel(seq, cap, cur, nxt, x_hbm, w1_ref, w3_ref, w2_ref, y_ref,
                       xs, xb, gate, acc, w1b, w3b, w2b, sem):
    e, h, f = pl.program_id(0), pl.program_id(1), pl.program_id(2)
    n_exp, n_half, nf = pl.num_programs(0), pl.num_programs(1), pl.num_programs(2)
    rows = MOE_PAIR * cap
    d = xb.shape[1]
    step = e * n_half + h
    last_step = n_exp * n_half - 1

    def start_row(idx_ref, half, s, pos):
        src = (half * MOE_PAIR + s) * seq + idx_ref[s, 0, pos]
        pltpu.make_async_copy(x_hbm.at[pl.ds(src, 1)], xs.at[pl.ds(s * cap + pos, 1)], sem.at[0]).start()

    def wait_rows():
        pltpu.make_async_copy(x_hbm.at[pl.ds(0, rows)], xs, sem.at[0]).wait()

    @pl.when((step == 0) & (f == 0))
    def _():
        for s in range(MOE_PAIR):
            lax.fori_loop(0, cap, lambda p, c: (start_row(cur, h, s, p), c)[1], 0)

    @pl.when(f == 0)
    def _():
        wait_rows()
        lane = lax.broadcasted_iota(I32, (MOE_CONV_ROWS, LANES), 1)
        for r in range(0, rows, MOE_CONV_ROWS):
            x = xs[r:r + MOE_CONV_ROWS, :]
            xb[r:r + MOE_CONV_ROWS, :] = x[:, :d].astype(BF16)
            gate[r:r + MOE_CONV_ROWS, :] = jnp.sum(jnp.where(lane == e, x[:, d:], 0.0), axis=1, keepdims=True)
        acc[...] = jnp.zeros_like(acc)

    per = rows // nf
    assert cap % per == 0
    nxt_h = jnp.minimum(step + 1, last_step) % n_half
    nxt_s = f // (cap // per)
    nxt_pos0 = (f % (cap // per)) * per
    for j in range(per):
        start_row(nxt, nxt_h, nxt_s, nxt_pos0 + j)

    for r in range(0, d, DOT_TM):
        w1b[r:r + DOT_TM, :] = w1_ref[r:r + DOT_TM, :].astype(BF16)
        w3b[r:r + DOT_TM, :] = w3_ref[r:r + DOT_TM, :].astype(BF16)
    for c in range(0, d, DOT_TN):
        w2b[:, c:c + DOT_TN] = w2_ref[:, c:c + DOT_TN].astype(BF16)

    for r in range(0, rows, DOT_TM):
        rs = slice(r, r + DOT_TM)
        xv = xb[rs, :]
        h1 = jnp.dot(xv, w1b[...], preferred_element_type=F32)
        h3 = jnp.dot(xv, w3b[...], preferred_element_type=F32)
        hh = (h1 / (1.0 + jnp.exp(-h1)) * h3).astype(BF16)
        for c in range(0, d, DOT_TN):
            acc[rs, c:c + DOT_TN] += jnp.dot(hh, w2b[:, c:c + DOT_TN], preferred_element_type=F32)

    @pl.when(f == nf - 1)
    def _():
        for r in range(0, rows, DOT_TM):
            y_ref[r:r + DOT_TM, :] = (acc[r:r + DOT_TM, :] * gate[r:r + DOT_TM, :]).astype(y_ref.dtype)

        @pl.when(step == last_step)
        def _():
            wait_rows()


def _moe_experts(x1e, idx, w1, w3, w2, layer, batch, seq, cap, tf):
    _, n_exp, d, ff = w1.shape
    n_half = batch // MOE_PAIR
    rows = MOE_PAIR * cap
    width = x1e.shape[1]

    def nxt(e, h):
        s = jnp.minimum(e * n_half + h + 1, n_exp * n_half - 1)
        return s // n_half, s % n_half

    idx_spec = lambda fn: pl.BlockSpec((None, MOE_PAIR, 1, cap), fn, memory_space=pltpu.SMEM)
    return pl.pallas_call(
        functools.partial(_moe_expert_kernel, seq, cap),
        grid=(n_exp, n_half, ff // tf),
        in_specs=[
            idx_spec(lambda e, h, f: (e, h, 0, 0)),
            idx_spec(lambda e, h, f: nxt(e, h) + (0, 0)),
            pl.BlockSpec(memory_space=pl.ANY),
            pl.BlockSpec((None, None, d, tf), lambda e, h, f: (layer, e, 0, f)),
            pl.BlockSpec((None, None, d, tf), lambda e, h, f: (layer, e, 0, f)),
            pl.BlockSpec((None, None, tf, d), lambda e, h, f: (layer, e, f, 0)),
        ],
        out_specs=pl.BlockSpec((rows, d), lambda e, h, f: (e * n_half + h, 0)),
        out_shape=jax.ShapeDtypeStruct((n_exp * batch * cap, d), BF16),
        scratch_shapes=[
            pltpu.VMEM((rows, width), F32),
            pltpu.VMEM((rows, d), BF16),
            pltpu.VMEM((rows, 1), F32),
            pltpu.VMEM((rows, d), F32),
            pltpu.VMEM((d, tf), BF16),
            pltpu.VMEM((d, tf), BF16),
            pltpu.VMEM((tf, d), BF16),
            pltpu.SemaphoreType.DMA((1,)),
        ],
        compiler_params=_cparams("arbitrary", "arbitrary", "arbitrary"),
        name="moe_experts",
    )(idx, idx, x1e, w1, w3, w2)


def _moe_combine_kernel(alpha, n_exp, batch, cap, tab_ref, y_hbm, tok_hbm, x_ref, g_ref, b_ref, o_ref,
                        ybuf, tokbuf, acc, sem):
    b, i = pl.program_id(0), pl.program_id(1)
    tm = x_ref.shape[0]

    @pl.when((b == 0) & (i == 0))
    def _():
        ybuf[...] = jnp.zeros_like(ybuf)
        tokbuf[...] = jnp.zeros_like(tokbuf)

    def window_copies(src_y, src_t, off):
        dst = pl.ds(pl.multiple_of(off * MOE_WIN, MOE_WIN), MOE_WIN)
        return (pltpu.make_async_copy(y_hbm.at[pl.ds(pl.multiple_of(src_y, MOE_WIN), MOE_WIN)], ybuf.at[dst], sem.at[0]),
                pltpu.make_async_copy(tok_hbm.at[pl.ds(pl.multiple_of(src_t, MOE_WIN), MOE_WIN)], tokbuf.at[dst], sem.at[1]))

    def per_expert(e, off):
        base = (b * n_exp + e) * MOE_TAB + i
        s0, s1 = tab_ref[base], tab_ref[base + 1]
        w0 = s0 // MOE_WIN
        nw = jnp.where(s1 > s0, (s1 + MOE_WIN - 1) // MOE_WIN - w0, 0)

        def per_window(w, off2):
            for cp in window_copies((e * batch + b) * cap + (w0 + w) * MOE_WIN,
                                    (b * n_exp + e) * cap + (w0 + w) * MOE_WIN, off2):
                cp.start()
            return off2 + 1

        return lax.fori_loop(0, nw, per_window, off)

    nwin = lax.fori_loop(0, n_exp, per_expert, 0)

    def wait_window(w, c):
        for cp in window_copies(0, 0, w):
            cp.wait()
        return c

    lax.fori_loop(0, nwin, wait_window, 0)

    nrows = nwin * MOE_WIN
    acc[...] = jnp.zeros_like(acc)
    sub = lax.broadcasted_iota(I32, (MOE_TILE, LANES), 0)
    lane = lax.broadcasted_iota(I32, (MOE_TILE, LANES), 1)

    def chunk(c, carry):
        rows = pl.ds(pl.multiple_of(c * MOE_TILE, MOE_TILE), MOE_TILE)
        t_local = tokbuf[rows, :] - i * tm
        valid = (sub + c * MOE_TILE) < nrows
        halves = [jnp.where(valid & (t_local == lane + k * LANES), 1.0, 0.0) for k in range(tm // LANES)]
        onehot = jnp.transpose(jnp.concatenate(halves, axis=1)).astype(BF16)
        for cc in range(0, acc.shape[1], DOT_TN):
            acc[:, cc:cc + DOT_TN] += jnp.dot(onehot, ybuf[rows, cc:cc + DOT_TN], preferred_element_type=F32)
        return carry

    lax.fori_loop(0, (nrows + MOE_TILE - 1) // MOE_TILE, chunk, 0)
    o_ref[...] = _layer_norm_rows(alpha * x_ref[...] + acc[...], g_ref[...], b_ref[...])


def _moe_combine(x1e, y, tok, tab, g, b, alpha, batch, seq, cap, n_exp):
    t = x1e.shape[0]
    d = y.shape[1]
    tm = MOE_TILE
    n_tiles = seq // tm
    max_rows = n_exp * (tm + MOE_WIN)
    max_rows = (max_rows + MOE_TILE - 1) // MOE_TILE * MOE_TILE
    return pl.pallas_call(
        functools.partial(_moe_combine_kernel, alpha, n_exp, batch, cap),
        grid_spec=pltpu.PrefetchScalarGridSpec(
            num_scalar_prefetch=1,
            grid=(batch, n_tiles),
            in_specs=[
                pl.BlockSpec(memory_space=pl.ANY),
                pl.BlockSpec(memory_space=pl.ANY),
                pl.BlockSpec((tm, d), lambda bb, i, tab_r: (bb * n_tiles + i, 0)),
                pl.BlockSpec((1, d), lambda bb, i, tab_r: (0, 0)),
                pl.BlockSpec((1, d), lambda bb, i, tab_r: (0, 0)),
            ],
            out_specs=pl.BlockSpec((tm, d), lambda bb, i, tab_r: (bb * n_tiles + i, 0)),
            scratch_shapes=[
                pltpu.VMEM((max_rows, d), BF16),
                pltpu.VMEM((max_rows, LANES), I32),
                pltpu.VMEM((tm, d), F32),
                pltpu.SemaphoreType.DMA((2,)),
            ],
        ),
        out_shape=jax.ShapeDtypeStruct((t, d), F32),
        compiler_params=_cparams("arbitrary", "arbitrary"),
        name="moe_combine",
    )(tab.reshape(-1), y, tok, x1e, g, b)


def _moe_block(x1e, afft, g, b, w1, w3, w2, layer, alpha, batch, seq):
    n_exp = w1.shape[1]
    cap = EC_CAPACITY_FACTOR * seq // n_exp
    idx, tok, tab = _moe_select(afft, batch, seq, cap)
    y = _moe_experts(x1e, idx, w1, w3, w2, layer, batch, seq, cap, tf=256)
    return _moe_combine(x1e, y, tok, tab, g, b, alpha, batch, seq, cap, n_exp)


def kernel(x, mix_w_in, mix_w_out, hy_conv_w, hy_conv_b, hy_ffn_w1, hy_ffn_b1, hy_ffn_w2, hy_ffn_b2, hy_ffn_w3,
           hy_ffn_b3, hy_ffn_w4, hy_sin_freq, hy_bias, pool_w, pool_scale, ln_mix_g, ln_mix_b, ln_ffn_g, ln_ffn_b,
           router_w, exp_w1, exp_w3, exp_w2):
    batch, seq, d = x.shape
    depth = ln_mix_g.shape[0]
    alpha = (2 * depth) ** 0.25
    n_exp = router_w.shape[2]
    xt = x.reshape(batch * seq, d)
    row = lambda v: v[None, :]
    for layer in range(depth):
        i = layer // 2
        rw_pad = jnp.pad(router_w[layer], ((0, 0), (0, LANES - n_exp)))
        if layer % 2 == 0:
            proj = _matmul([xt], mix_w_in[i], BF16, 512, 1024)
            attn = _dilated_attention(proj, batch, seq)
            hcat = _hyena_filters(seq, hy_ffn_w1[i], hy_ffn_b1[i], hy_ffn_w2[i], hy_ffn_b2[i], hy_ffn_w3[i],
                                  hy_ffn_b3[i], hy_ffn_w4[i], hy_sin_freq[i])
            z, x0c = _hyena_pre(proj, hy_conv_w[i], hy_conv_b[i], seq)
            cosm, nsinm = _dft_matrices(seq)
            k_re, k_im = _dft_filter(cosm, nsinm, hcat, 2048)
            p_re, p_im = _dft_fwd(cosm, nsinm, z, k_re, k_im, batch, 2048)
            hyena = _dft_inv(cosm, nsinm, p_re, p_im, z, x0c, hy_bias[i], batch, 1024, 2048)
            mixed = _matmul([attn, hyena], mix_w_out[i], F32, 512, 1024)
            x1e, afft = _ln_router(xt, mixed, row(ln_mix_g[layer]), row(ln_mix_b[layer]), rw_pad, alpha, n_exp)
        else:
            x1e, afft = _pool_ln_router(xt, pool_w[i], row(pool_scale[i]), row(ln_mix_g[layer]), row(ln_mix_b[layer]),
                                        rw_pad, alpha, n_exp, seq)
        xt = _moe_block(x1e, afft, row(ln_ffn_g[layer]), row(ln_ffn_b[layer]), exp_w1, exp_w3, exp_w2, layer,
                        alpha, batch, seq)
    return xt.reshape(batch, seq, d)
```

```python
import functools
import math

import numpy as np
import jax
import jax.numpy as jnp
from jax import lax
from jax.experimental import pallas as pl
from jax.experimental.pallas import tpu as pltpu

F32 = jnp.float32
BF16 = jnp.bfloat16
I32 = jnp.int32

LANES = 128
SUBLANES = 8
VMEM_LIMIT_BYTES = 56 * 1024 * 1024

ATTN_HEADS = 8
HEAD_DIM = 128
DILATED_BRANCHES = ((128, 1), (512, 4), (2048, 16))
POOL_WINDOWS = (2, 4, 8, 16)
N_EXPERTS = 16
EC_CAPACITY_FACTOR = 2
HYENA_POS_DIM = 33
HYENA_DECAY_FAST = 0.3
HYENA_DECAY_SLOW = 1.5
HYENA_DECAY_TARGET = 1e-2
LN_EPS = 1e-5
NEG_BIG = -1e30


def _cparams(*sem):
    return pltpu.CompilerParams(dimension_semantics=sem, vmem_limit_bytes=VMEM_LIMIT_BYTES)


DOT_TM = 256
DOT_TN = 512


def _dot_tiles(a_ref, b_ref, emit, cast=False):
    m, n = a_ref.shape[0], b_ref.shape[1]
    tm, tn = min(DOT_TM, m), min(DOT_TN, n)
    for r in range(0, m, tm):
        a = a_ref[r:r + tm, :]
        a = a.astype(BF16) if cast else a
        for c in range(0, n, tn):
            emit(slice(r, r + tm), slice(c, c + tn), jnp.dot(a, b_ref[:, c:c + tn], preferred_element_type=F32))


def _mm_kernel(n_in, *refs):
    x_refs, w_ref, o_ref, wb_ref = refs[:n_in], refs[n_in], refs[n_in + 1], refs[n_in + 2]

    @pl.when(pl.program_id(1) == 0)
    def _():
        for r in range(0, w_ref.shape[0], DOT_TM):
            wb_ref[r:r + DOT_TM, :] = w_ref[r:r + DOT_TM, :].astype(BF16)

    tm, tn = x_refs[0].shape[0], o_ref.shape[1]
    for r in range(0, tm, DOT_TM):
        xs = [x_ref[r:r + DOT_TM, :].astype(BF16) for x_ref in x_refs]
        for c in range(0, tn, DOT_TN):
            k0, v = 0, None
            for xv in xs:
                part = jnp.dot(xv, wb_ref[k0:k0 + xv.shape[1], c:c + DOT_TN], preferred_element_type=F32)
                v = part if v is None else v + part
                k0 += xv.shape[1]
            o_ref[r:r + DOT_TM, c:c + DOT_TN] = v.astype(o_ref.dtype)


def _matmul(xs, w, out_dtype, tm, tn):
    m = xs[0].shape[0]
    k, n = w.shape
    assert sum(x.shape[1] for x in xs) == k
    return pl.pallas_call(
        functools.partial(_mm_kernel, len(xs)),
        grid=(n // tn, m // tm),
        in_specs=[pl.BlockSpec((tm, x.shape[1]), lambda j, i: (i, 0)) for x in xs]
        + [pl.BlockSpec((k, tn), lambda j, i: (0, j))],
        out_specs=pl.BlockSpec((tm, tn), lambda j, i: (i, j)),
        out_shape=jax.ShapeDtypeStruct((m, n), out_dtype),
        scratch_shapes=[pltpu.VMEM((k, tn), BF16)],
        compiler_params=_cparams("arbitrary", "arbitrary"),
        name="matmul",
    )(*xs, w)


def _layer_norm_rows(v, g, b):
    mu = jnp.mean(v, axis=-1, keepdims=True)
    c = v - mu
    var = jnp.mean(c * c, axis=-1, keepdims=True)
    return c * lax.rsqrt(var + LN_EPS) * g + b


def _bf16_pieces(v):
    hi = v.astype(BF16)
    return hi, (v - hi.astype(F32)).astype(BF16)


def _router_affinity(x1, rw_ref, n_experts):
    xh, xl = _bf16_pieces(x1)
    wh, wl = _bf16_pieces(rw_ref[...])
    dot = functools.partial(jnp.dot, preferred_element_type=F32)
    logits = dot(xh, wh) + (dot(xh, wl) + dot(xl, wh))
    lane = lax.broadcasted_iota(I32, logits.shape, 1)
    logits = jnp.where(lane < n_experts, logits, NEG_BIG)
    mx = jnp.max(logits, axis=-1, keepdims=True)
    e = jnp.exp(logits - mx)
    return e / jnp.sum(e, axis=-1, keepdims=True)


def _ln_router_kernel(alpha, n_experts, x_ref, m_ref, g_ref, b_ref, rw_ref, xe_ref, afft_ref):
    d = x_ref.shape[1]
    x1 = _layer_norm_rows(alpha * x_ref[...] + m_ref[...].astype(F32), g_ref[...], b_ref[...])
    aff = _router_affinity(x1, rw_ref, n_experts)
    xe_ref[:, :d] = x1
    xe_ref[:, d:] = aff
    afft_ref[...] = jnp.transpose(aff)[:n_experts, :]


def _ln_router(x, mixed, g, b, rw_pad, alpha, n_experts, tm=256):
    t, d = x.shape
    return pl.pallas_call(
        functools.partial(_ln_router_kernel, alpha, n_experts),
        grid=(t // tm,),
        in_specs=[
            pl.BlockSpec((tm, d), lambda i: (i, 0)),
            pl.BlockSpec((tm, d), lambda i: (i, 0)),
            pl.BlockSpec((1, d), lambda i: (0, 0)),
            pl.BlockSpec((1, d), lambda i: (0, 0)),
            pl.BlockSpec((d, LANES), lambda i: (0, 0)),
        ],
        out_specs=[pl.BlockSpec((tm, d + LANES), lambda i: (i, 0)), pl.BlockSpec((n_experts, tm), lambda i: (0, i))],
        out_shape=[jax.ShapeDtypeStruct((t, d + LANES), F32), jax.ShapeDtypeStruct((n_experts, t), F32)],
        compiler_params=_cparams("arbitrary"),
        name="ln_router",
    )(x, mixed, g, b, rw_pad)


ATTN_QBLK = 128
ATTN_UNROLL = 16
ATTN_HALF = 64
ATTN_WINDOW_LEADS = (0, ATTN_HALF, 2 * ATTN_HALF)


def _attn_branch(slope, dil, lc, q_cm, k_cm, v_cm, o_cm, l_cm, seq, bias_ref):
    win = min(ATTN_QBLK + 2 * ATTN_HALF, lc)
    blocks_per_class = lc // ATTN_QBLK
    scale = 1.0 / math.sqrt(HEAD_DIM)
    diff = lax.broadcasted_iota(I32, (ATTN_QBLK, win), 1) - lax.broadcasted_iota(I32, (ATTN_QBLK, win), 0)
    for case in range(len(ATTN_WINDOW_LEADS)):
        rel = jnp.abs(diff - ATTN_WINDOW_LEADS[case])
        bias_ref[case, :, :win] = jnp.where(rel <= ATTN_HALF, -(slope * dil) * rel.astype(F32), NEG_BIG)

    def body(blk, carry):
        cls = blk // blocks_per_class
        q0 = (blk % blocks_per_class) * ATTN_QBLK
        w0 = jnp.clip(q0 - ATTN_HALF, 0, lc - win)
        qs = pl.multiple_of(cls * lc + q0, ATTN_HALF)
        ks = pl.multiple_of(cls * lc + w0, ATTN_HALF)
        q = q_cm[pl.ds(qs, ATTN_QBLK), :]
        k = k_cm[pl.ds(ks, win), :]
        v = v_cm[pl.ds(ks, win), :]
        s = lax.dot_general(q, k, (((1,), (1,)), ((), ())), preferred_element_type=F32) * scale
        bias = bias_ref[(q0 - w0) // ATTN_HALF, :, :win]
        s = jnp.where(bias > 0.5 * NEG_BIG, s + bias, NEG_BIG)
        m = jnp.max(s, axis=-1, keepdims=True)
        p = jnp.exp(s - m)
        l = jnp.sum(p, axis=-1, keepdims=True)
        o = jnp.dot(p.astype(BF16), v, preferred_element_type=F32) / l
        o_cm[pl.ds(qs, ATTN_QBLK), :] = o
        l_cm[pl.ds(qs, ATTN_QBLK), :] = jnp.broadcast_to(m + jnp.log(l), (ATTN_QBLK, HEAD_DIM))
        return carry

    lax.fori_loop(0, seq // ATTN_QBLK, body, 0, unroll=ATTN_UNROLL)


def _attn_kernel(slopes_ref, q_ref, k_ref, v_ref, out_ref, qf, kf, vf, qc, kc, vc, ocm, lcm, onat, lnat, bias_ref):
    seq = q_ref.shape[0]
    slope = slopes_ref[pl.program_id(1)]
    qf[...] = q_ref[...].astype(F32)
    kf[...] = k_ref[...].astype(F32)
    vf[...] = v_ref[...].astype(F32)
    for bi, (window, dil) in enumerate(DILATED_BRANCHES):
        assert window // (2 * dil) == ATTN_HALF
        lc = seq // dil
        if dil == 1:
            _attn_branch(slope, dil, lc, q_ref, k_ref, v_ref, onat.at[bi], lnat.at[bi], seq, bias_ref)
            continue
        for r in range(dil):
            qc[r * lc:(r + 1) * lc, :] = qf[pl.ds(r, lc, stride=dil), :].astype(BF16)
            kc[r * lc:(r + 1) * lc, :] = kf[pl.ds(r, lc, stride=dil), :].astype(BF16)
            vc[r * lc:(r + 1) * lc, :] = vf[pl.ds(r, lc, stride=dil), :].astype(BF16)
        _attn_branch(slope, dil, lc, qc, kc, vc, ocm, lcm, seq, bias_ref)
        for r in range(dil):
            onat[bi, pl.ds(r, lc, stride=dil), :] = ocm[r * lc:(r + 1) * lc, :]
            lnat[bi, pl.ds(r, lc, stride=dil), :] = lcm[r * lc:(r + 1) * lc, :]
    lses = [lnat[bi] for bi in range(len(DILATED_BRANCHES))]
    mx = functools.reduce(jnp.maximum, lses)
    ws = [jnp.exp(l - mx) for l in lses]
    num = sum(w * onat[bi] for bi, w in enumerate(ws))
    out_ref[...] = (num / sum(ws)).astype(out_ref.dtype)


def _dilated_attention(proj, batch, seq):
    t = proj.shape[0]
    slopes = jnp.asarray(2.0 ** (-(8.0 / ATTN_HEADS) * np.arange(1, ATTN_HEADS + 1)), F32)
    blk = (seq, HEAD_DIM)
    nb = len(DILATED_BRANCHES)
    return pl.pallas_call(
        _attn_kernel,
        grid_spec=pltpu.PrefetchScalarGridSpec(
            num_scalar_prefetch=1,
            grid=(batch, ATTN_HEADS),
            in_specs=[
                pl.BlockSpec(blk, lambda b, h, s: (b, h)),
                pl.BlockSpec(blk, lambda b, h, s: (b, ATTN_HEADS + h)),
                pl.BlockSpec(blk, lambda b, h, s: (b, 2 * ATTN_HEADS + h)),
            ],
            out_specs=pl.BlockSpec(blk, lambda b, h, s: (b, h)),
            scratch_shapes=[pltpu.VMEM(blk, F32)] * 3 + [pltpu.VMEM(blk, BF16)] * 3 + [pltpu.VMEM(blk, F32)] * 2
            + [pltpu.VMEM((nb,) + blk, F32)] * 2
            + [pltpu.VMEM((len(ATTN_WINDOW_LEADS), ATTN_QBLK, ATTN_QBLK + 2 * ATTN_HALF), F32)],
        ),
        out_shape=jax.ShapeDtypeStruct((t, ATTN_HEADS * HEAD_DIM), BF16),
        compiler_params=_cparams("arbitrary", "arbitrary"),
        name="dilated_attention",
    )(slopes, proj, proj, proj)


HYENA_ORDER_PAD = LANES


def _filter_kernel(z_ref, w1, b1, w2, b2, w3, b3, w4, fr, absdelta_ref, h_ref):
    hi = lax.Precision.HIGHEST
    f = fr[...]
    z = z_ref[...]
    h = jnp.sin(f * (jnp.dot(z, w1[...], preferred_element_type=F32, precision=hi) + b1[...]))
    h = jnp.sin(f * (jnp.dot(h, w2[...], preferred_element_type=F32, precision=hi) + b2[...]))
    h = jnp.sin(f * (jnp.dot(h, w3[...], preferred_element_type=F32, precision=hi) + b3[...]))
    h = jnp.dot(h, w4[...], preferred_element_type=F32, precision=hi)
    t = z[:, 0:1]
    h = h * jnp.exp(-t * absdelta_ref[...])
    c = h.shape[1] // 2
    row = lax.broadcasted_iota(I32, h.shape, 0) + pl.program_id(0) * h.shape[0]
    col = lax.broadcasted_iota(I32, h.shape, 1)
    h_ref[...] = jnp.where((row == 0) & (col >= c), 0.0, h).astype(h_ref.dtype)


def _position_features(seq):
    t = np.linspace(0.0, 1.0, seq)[:, None]
    bands = (HYENA_POS_DIM - 1) // 2
    w_ang = 2.0 * np.pi * np.arange(seq)[:, None] / seq
    f = np.linspace(1e-4, bands - 1, bands)[None, :]
    z = np.concatenate([t, np.cos(f * w_ang), -np.sin(f * w_ang)], axis=-1)
    return np.pad(z, ((0, 0), (0, HYENA_ORDER_PAD - z.shape[1]))).astype(np.float32)


def _pad2(a, rows, cols):
    return jnp.pad(a, ((0, rows - a.shape[0]), (0, cols - a.shape[1])))


def _hyena_filters(seq, w1, b1, w2, b2, w3, b3, w4, freq, tl=512):
    p = HYENA_ORDER_PAD
    c2 = w4.shape[1]
    c = c2 // 2
    max_decay = math.log(HYENA_DECAY_TARGET) / HYENA_DECAY_FAST
    min_decay = math.log(HYENA_DECAY_TARGET) / HYENA_DECAY_SLOW
    absdelta = np.abs(np.linspace(min_decay, max_decay, c))
    absdelta = jnp.asarray(np.concatenate([absdelta, absdelta])[None, :], F32)
    args = (
        jnp.asarray(_position_features(seq)),
        _pad2(w1, p, p), _pad2(b1[None, :], 1, p), _pad2(w2, p, p), _pad2(b2[None, :], 1, p),
        _pad2(w3, p, p), _pad2(b3[None, :], 1, p), _pad2(w4, p, c2), _pad2(freq[None, :], 1, p), absdelta,
    )
    full = lambda a: pl.BlockSpec(a.shape, lambda i: (0, 0))
    return pl.pallas_call(
        _filter_kernel,
        grid=(seq // tl,),
        in_specs=[pl.BlockSpec((tl, p), lambda i: (i, 0))] + [full(a) for a in args[1:]],
        out_specs=pl.BlockSpec((tl, c2), lambda i: (i, 0)),
        out_shape=jax.ShapeDtypeStruct((seq, c2), BF16),
        compiler_params=_cparams("arbitrary"),
        name="hyena_filters",
    )(*args)


def _shift_rows(u, edge_row, down):
    n = u.shape[0]
    row = lax.broadcasted_iota(I32, u.shape, 0)
    if down:
        return jnp.where(row == 0, edge_row, pltpu.roll(u, 1, 0))
    return jnp.where(row == n - 1, edge_row, pltpu.roll(u, n - 1, 0))


def _hyena_pre_kernel(tiles_per_seq, cur_ref, prev_ref, next_ref, w_ref, b_ref, z_ref, x0_ref):
    i = pl.program_id(0) % tiles_per_seq
    u = cur_ref[...].astype(F32)
    prev_row = jnp.where(i == 0, 0.0, prev_ref[SUBLANES - 1:SUBLANES, :].astype(F32))
    next_row = jnp.where(i == tiles_per_seq - 1, 0.0, next_ref[0:1, :].astype(F32))
    y = (w_ref[0:1, :] * _shift_rows(u, prev_row, True) + w_ref[1:2, :] * u
         + w_ref[2:3, :] * _shift_rows(u, next_row, False) + b_ref[...])
    c = y.shape[1] // 3
    x0_ref[...] = y[:, :c].astype(x0_ref.dtype)
    z_ref[...] = (y[:, 2 * c:] * y[:, c:2 * c]).astype(z_ref.dtype)


def _hyena_pre(proj, conv_w, conv_b, seq, ts=256):
    t, width = proj.shape
    c3 = conv_w.shape[1]
    assert width == 2 * c3
    c = c3 // 3
    tiles_per_seq = seq // ts
    r8 = ts // SUBLANES
    last8 = t // SUBLANES - 1
    return pl.pallas_call(
        functools.partial(_hyena_pre_kernel, tiles_per_seq),
        grid=(t // ts,),
        in_specs=[
            pl.BlockSpec((ts, c3), lambda i: (i, 1)),
            pl.BlockSpec((SUBLANES, c3), lambda i: (jnp.maximum(i * r8 - 1, 0), 1)),
            pl.BlockSpec((SUBLANES, c3), lambda i: (jnp.minimum((i + 1) * r8, last8), 1)),
            pl.BlockSpec((3, c3), lambda i: (0, 0)),
            pl.BlockSpec((1, c3), lambda i: (0, 0)),
        ],
        out_specs=[pl.BlockSpec((ts, c), lambda i: (i, 0))] * 2,
        out_shape=[jax.ShapeDtypeStruct((t, c), BF16)] * 2,
        compiler_params=_cparams("arbitrary"),
        name="hyena_short_conv",
    )(proj, proj, proj, conv_w, conv_b[None, :])


DFT_FBLK = 512
DFT_SPLIT = 64
DFT_GEN_ROWS = 256
DFT_EPI_ROWS = 64
BF16_ROWS = 2 * SUBLANES


def _dft_tables(seq):
    n = 2 * seq
    k = np.arange(seq)[:, None]
    a = np.arange(seq // DFT_SPLIT)[None, :]
    b = np.arange(DFT_SPLIT)[None, :]
    ang1 = 2.0 * np.pi * ((k * a * DFT_SPLIT) % n) / n
    ang2 = 2.0 * np.pi * ((k * b) % n) / n
    return tuple(jnp.asarray(f(a_), F32) for a_ in (ang1, ang2) for f in (np.cos, np.sin))


def _dft_gen_kernel(c1_ref, s1_ref, c2_ref, s2_ref, cos_ref, nsin_ref):
    rows, na = c1_ref.shape
    seq = cos_ref.shape[1]
    reps = DOT_TN // DFT_SPLIT
    c2 = jnp.concatenate([c2_ref[...]] * reps, axis=1)
    s2 = jnp.concatenate([s2_ref[...]] * reps, axis=1)

    c1p, s1p = _bf16_pieces(c1_ref[...]), _bf16_pieces(s1_ref[...])
    for c in range(0, seq, DOT_TN):
        n_of_lane = lax.broadcasted_iota(I32, (na, DOT_TN), 1) + c
        expand = jnp.where(n_of_lane // DFT_SPLIT == lax.broadcasted_iota(I32, (na, DOT_TN), 0), 1.0, 0.0).astype(BF16)
        a, b = (jnp.dot(hi, expand, preferred_element_type=F32) + jnp.dot(lo, expand, preferred_element_type=F32)
                for hi, lo in (c1p, s1p))
        cos_ref[:, c:c + DOT_TN] = (a * c2 - b * s2).astype(cos_ref.dtype)
        nsin_ref[:, c:c + DOT_TN] = (-(b * c2 + a * s2)).astype(nsin_ref.dtype)


def _dft_matrices(seq):
    tables = _dft_tables(seq)
    tr = DFT_GEN_ROWS
    return pl.pallas_call(
        _dft_gen_kernel,
        grid=(seq // tr,),
        in_specs=[pl.BlockSpec((tr, t.shape[1]), lambda i: (i, 0)) for t in tables],
        out_specs=[pl.BlockSpec((tr, seq), lambda i: (i, 0))] * 2,
        out_shape=[jax.ShapeDtypeStruct((seq, seq), BF16)] * 2,
        compiler_params=_cparams("arbitrary"),
        name="dft_matrices",
    )(*tables)


def _nyquist_rows(tk):
    lane = lax.broadcasted_iota(I32, (SUBLANES, tk), 1)
    return jnp.where(lane % 2 == 0, 1.0, -1.0).astype(BF16)


def _dft_accumulate(cos_ref, nsin_ref, u_ref, re_ref, im_ref, ny_ref, kk, first_block):
    @pl.when(kk == 0)
    def _():
        re_ref[...] = jnp.zeros_like(re_ref)
        im_ref[...] = jnp.zeros_like(im_ref)
        ny_ref[...] = jnp.zeros_like(ny_ref)

    def add_re(rows, cols, v):
        re_ref[rows, cols] += v

    def add_im(rows, cols, v):
        im_ref[rows, cols] += v

    _dot_tiles(cos_ref, u_ref, add_re)
    _dot_tiles(nsin_ref, u_ref, add_im)

    @pl.when(first_block)
    def _():
        ny_ref[...] += jnp.dot(_nyquist_rows(u_ref.shape[0]), u_ref[...], preferred_element_type=F32)


def _dft_filter_kernel(cos_ref, nsin_ref, h_ref, kre_ref, kim_ref, re_ref, im_ref, ny_ref):
    i, kk = pl.program_id(0), pl.program_id(1)
    _dft_accumulate(cos_ref, nsin_ref, h_ref, re_ref, im_ref, ny_ref, kk, i == 0)

    @pl.when(kk == pl.num_programs(1) - 1)
    def _():
        c = kre_ref.shape[1]
        kre_ref[...] = re_ref[:, :c] + re_ref[:, c:]
        kim_ref[...] = im_ref[:, :c] - im_ref[:, c:]

        @pl.when(i == 0)
        def _():
            kim_ref[0:1, :] = ny_ref[0:1, :c] + ny_ref[0:1, c:]


def _dft_filter(cosm, nsinm, hcat, tk):
    seq = cosm.shape[0]
    c2 = hcat.shape[1]
    mat = pl.BlockSpec((DFT_FBLK, tk), lambda i, kk: (i, kk))
    return pl.pallas_call(
        _dft_filter_kernel,
        grid=(seq // DFT_FBLK, seq // tk),
        in_specs=[mat, mat, pl.BlockSpec((tk, c2), lambda i, kk: (kk, 0))],
        out_specs=[pl.BlockSpec((DFT_FBLK, c2 // 2), lambda i, kk: (i, 0))] * 2,
        out_shape=[jax.ShapeDtypeStruct((seq, c2 // 2), F32)] * 2,
        scratch_shapes=[pltpu.VMEM((DFT_FBLK, c2), F32)] * 2 + [pltpu.VMEM((SUBLANES, c2), F32)],
        compiler_params=_cparams("arbitrary", "arbitrary"),
        name="dft_filter",
    )(cosm, nsinm, hcat)


def _dft_fwd_kernel(seq, cos_ref, nsin_ref, z_ref, kre_ref, kim_ref, pre_ref, pim_ref, re_ref, im_ref, ny_ref):
    i, kk = pl.program_id(1), pl.program_id(2)
    _dft_accumulate(cos_ref, nsin_ref, z_ref, re_ref, im_ref, ny_ref, kk, i == 0)

    @pl.when(kk == pl.num_programs(2) - 1)
    def _():
        inv_n = 1.0 / (2 * seq)
        for r in range(0, DFT_FBLK, DFT_EPI_ROWS):
            rs = slice(r, r + DFT_EPI_ROWS)
            ure, uim, kre, kim = re_ref[rs, :], im_ref[rs, :], kre_ref[rs, :], kim_ref[rs, :]
            pre_ref[rs, :] = ((ure * kre - uim * kim) * (2 * inv_n)).astype(pre_ref.dtype)
            pim_ref[rs, :] = ((ure * kim + uim * kre) * (2 * inv_n)).astype(pim_ref.dtype)

        @pl.when(i == 0)
        def _():
            head = slice(0, BF16_ROWS)
            ure, uim, kre, kim = re_ref[head, :], im_ref[head, :], kre_ref[head, :], kim_ref[head, :]
            first = lax.broadcasted_iota(I32, ure.shape, 0) == 0
            p_re = jnp.where(first, ure * kre * inv_n, (ure * kre - uim * kim) * (2 * inv_n))
            p_im = jnp.where(first, ny_ref[0:1, :] * kim * inv_n, (ure * kim + uim * kre) * (2 * inv_n))
            pre_ref[head, :] = p_re.astype(pre_ref.dtype)
            pim_ref[head, :] = p_im.astype(pim_ref.dtype)


def _dft_fwd(cosm, nsinm, z, kre, kim, batch, tk):
    seq = cosm.shape[0]
    c = z.shape[1]
    nfb = seq // DFT_FBLK
    nk = seq // tk
    mat = pl.BlockSpec((DFT_FBLK, tk), lambda b, i, kk: (i, kk))
    spec = pl.BlockSpec((DFT_FBLK, c), lambda b, i, kk: (i, 0))
    return pl.pallas_call(
        functools.partial(_dft_fwd_kernel, seq),
        grid=(batch, nfb, nk),
        in_specs=[mat, mat, pl.BlockSpec((tk, c), lambda b, i, kk: (b * nk + kk, 0)), spec, spec],
        out_specs=[pl.BlockSpec((DFT_FBLK, c), lambda b, i, kk: (b * nfb + i, 0))] * 2,
        out_shape=[jax.ShapeDtypeStruct((batch * seq, c), BF16)] * 2,
        scratch_shapes=[pltpu.VMEM((DFT_FBLK, c), F32)] * 2 + [pltpu.VMEM((SUBLANES, c), F32)],
        compiler_params=_cparams("arbitrary", "arbitrary", "arbitrary"),
        name="dft_fwd",
    )(cosm, nsinm, z, kre, kim)


def _dft_inv_kernel(cos_ref, nsin_ref, pre_ref, pim_ref, pny_ref, z_ref, x0_ref, bias_ref, o_ref, acc_ref):
    it, kk = pl.program_id(1), pl.program_id(2)
    tt = acc_ref.shape[0]

    @pl.when(kk == 0)
    def _():
        acc_ref[...] = jnp.zeros_like(acc_ref)

    def accumulate(rows, cols, v):
        acc_ref[rows, cols] += v

    _dot_tiles(cos_ref, pre_ref, accumulate)
    _dot_tiles(nsin_ref, pim_ref, accumulate)

    @pl.when(kk == pl.num_programs(2) - 1)
    def _():
        nyq = pny_ref[0:1, :].astype(F32)
        for r in range(0, tt, DOT_TM):
            rs = slice(r, r + DOT_TM)
            t = it * tt + r + lax.broadcasted_iota(I32, (DOT_TM, 1), 0)
            y = acc_ref[rs, :] + jnp.where(t % 2 == 0, 1.0, -1.0) * nyq
            zz = z_ref[rs, :].astype(F32)
            o_ref[rs, :] = (x0_ref[rs, :].astype(F32) * (y + zz * bias_ref[...])).astype(o_ref.dtype)


def _dft_inv(cosm, nsinm, p_re, p_im, z, x0c, hy_bias, batch, tt, tk):
    seq = cosm.shape[0]
    c = z.shape[1]
    nt = seq // tt
    nk = seq // tk
    mat = pl.BlockSpec((tt, tk), lambda b, i, kk: (i, kk))
    spec = pl.BlockSpec((tk, c), lambda b, i, kk: (b * nk + kk, 0))
    tile = pl.BlockSpec((tt, c), lambda b, i, kk: (b * nt + i, 0))
    return pl.pallas_call(
        _dft_inv_kernel,
        grid=(batch, nt, nk),
        in_specs=[mat, mat, spec, spec, pl.BlockSpec((BF16_ROWS, c), lambda b, i, kk: (b * (seq // BF16_ROWS), 0)),
                  tile, tile, pl.BlockSpec((1, c), lambda b, i, kk: (0, 0))],
        out_specs=tile,
        out_shape=jax.ShapeDtypeStruct((batch * seq, c), BF16),
        scratch_shapes=[pltpu.VMEM((tt, c), F32)],
        compiler_params=_cparams("arbitrary", "arbitrary", "arbitrary"),
        name="dft_inv",
    )(cosm, nsinm, p_re, p_im, p_im, z, x0c, hy_bias[None, :])


POOL_HALO = SUBLANES


def _pool_ln_router_kernel(alpha, n_experts, seq, x_ref, prev_ref, next_ref, pw_ref, ps_ref, g_ref, b_ref, rw_ref,
                           xe_ref, afft_ref, ext_ref, mix_ref):
    tm, d = x_ref.shape
    tiles_per_seq = seq // tm
    i = pl.program_id(0) % tiles_per_seq
    ext_ref[:POOL_HALO, :] = jnp.where(i == 0, 0.0, prev_ref[...])
    ext_ref[POOL_HALO:POOL_HALO + tm, :] = x_ref[...]
    ext_ref[POOL_HALO + tm:, :] = jnp.where(i == tiles_per_seq - 1, 0.0, next_ref[...])
    pos = i * tm + lax.broadcasted_iota(I32, (tm, 1), 0)
    group = d // len(POOL_WINDOWS)
    for gi, win in enumerate(POOL_WINDOWS):
        half = win // 2
        cols = slice(gi * group, (gi + 1) * group)
        wsum = ext_ref[POOL_HALO - half:POOL_HALO - half + tm, cols]
        for j in range(1 - half, half):
            wsum = wsum + ext_ref[POOL_HALO + j:POOL_HALO + j + tm, cols]
        count = (jnp.minimum(pos + half, seq) - jnp.maximum(pos - half, 0)).astype(F32)
        dev = wsum / count - x_ref[:, cols]
        mix_ref[:, cols] = jnp.dot(dev.astype(BF16), pw_ref[gi].astype(BF16), preferred_element_type=F32)
    x1 = _layer_norm_rows(alpha * x_ref[...] + mix_ref[...] * ps_ref[...], g_ref[...], b_ref[...])
    aff = _router_affinity(x1, rw_ref, n_experts)
    xe_ref[:, :d] = x1
    xe_ref[:, d:] = aff
    afft_ref[...] = jnp.transpose(aff)[:n_experts, :]


def _pool_ln_router(x, pool_w, pool_scale, g, b, rw_pad, alpha, n_experts, seq, tm=256):
    t, d = x.shape
    r8 = tm // SUBLANES
    last8 = t // SUBLANES - 1
    const = lambda a: pl.BlockSpec(a.shape, lambda i: (0,) * a.ndim)
    return pl.pallas_call(
        functools.partial(_pool_ln_router_kernel, alpha, n_experts, seq),
        grid=(t // tm,),
        in_specs=[
            pl.BlockSpec((tm, d), lambda i: (i, 0)),
            pl.BlockSpec((SUBLANES, d), lambda i: (jnp.maximum(i * r8 - 1, 0), 0)),
            pl.BlockSpec((SUBLANES, d), lambda i: (jnp.minimum((i + 1) * r8, last8), 0)),
            const(pool_w), const(pool_scale), const(g), const(b), const(rw_pad),
        ],
        out_specs=[pl.BlockSpec((tm, d + LANES), lambda i: (i, 0)), pl.BlockSpec((n_experts, tm), lambda i: (0, i))],
        out_shape=[jax.ShapeDtypeStruct((t, d + LANES), F32), jax.ShapeDtypeStruct((n_experts, t), F32)],
        scratch_shapes=[pltpu.VMEM((tm + 2 * POOL_HALO, d), F32), pltpu.VMEM((tm, d), F32)],
        compiler_params=_cparams("arbitrary"),
        name="pool_ln_router",
    )(x, x, x, pool_w, pool_scale, g, b, rw_pad)


MOE_TILE = 256
MOE_TAB = 32
MOE_JBLK = 128
MOE_WIN = 16


def _lane_cumsum(mask_f, tri):
    rows, n = mask_f.shape
    run = jnp.zeros((rows, 1), F32)
    parts, starts = [], []
    for c in range(n // MOE_TILE):
        starts.append(run)
        m = mask_f[:, c * MOE_TILE:(c + 1) * MOE_TILE].astype(BF16)
        cs = jnp.dot(m, tri, preferred_element_type=F32) + run
        parts.append(cs)
        run = cs[:, MOE_TILE - 1:MOE_TILE]
    starts.append(run)
    return jnp.concatenate(parts, axis=1), starts


def _moe_select_kernel(cap, afft_ref, idx_ref, tok_ref, tab_ref, csel_ref):
    n_exp, seq = afft_ref.shape
    aff = afft_ref[...]

    def search(it, thr_bits):
        cand = thr_bits | jnp.left_shift(jnp.int32(1), 30 - it)
        cnt = jnp.sum(jnp.where(aff >= pltpu.bitcast(cand, F32), 1.0, 0.0), axis=1, keepdims=True)
        return jnp.where(cnt >= cap, cand, thr_bits)

    thr = pltpu.bitcast(lax.fori_loop(0, 31, search, jnp.zeros((n_exp, 1), I32)), F32)
    gt = aff > thr
    eq = aff == thr
    need = cap - jnp.sum(jnp.where(gt, 1.0, 0.0), axis=1, keepdims=True)
    r = lax.broadcasted_iota(I32, (MOE_TILE, MOE_TILE), 0)
    c = lax.broadcasted_iota(I32, (MOE_TILE, MOE_TILE), 1)
    tri = jnp.where(r <= c, 1.0, 0.0).astype(BF16)
    ceq, _ = _lane_cumsum(jnp.where(eq, 1.0, 0.0), tri)
    sel = jnp.where(gt, 1.0, jnp.where(eq & (ceq <= need), 1.0, 0.0))
    csel, starts = _lane_cumsum(sel, tri)
    csel_ref[...] = csel
    lane = lax.broadcasted_iota(I32, (n_exp, MOE_TAB), 1)
    tab = jnp.zeros((n_exp, MOE_TAB), I32)
    for ti, s in enumerate(starts):
        tab = jnp.where(lane == ti, s.astype(I32), tab)
    tab_ref[...] = tab

    def per_expert(e, carry):
        row = csel_ref[pl.ds(e, 1), :]
        for jb in range(cap // MOE_JBLK):
            jcol = (lax.broadcasted_iota(I32, (MOE_JBLK, LANES), 0) + jb * MOE_JBLK).astype(F32)
            acc = jnp.zeros((MOE_JBLK, LANES), F32)
            for tc in range(seq // LANES):
                acc = acc + jnp.where(row[:, tc * LANES:(tc + 1) * LANES] <= jcol, 1.0, 0.0)
            tokcol = jnp.sum(acc, axis=1, keepdims=True).astype(I32)
            tokb = jnp.broadcast_to(tokcol, (MOE_JBLK, LANES))
            tok_ref[pl.ds(pl.multiple_of(e * cap + jb * MOE_JBLK, MOE_JBLK), MOE_JBLK), :] = tokb
            idx_ref[e, :, jb * MOE_JBLK:(jb + 1) * MOE_JBLK] = jnp.transpose(tokb)[0:1, :]
        return carry

    lax.fori_loop(0, n_exp, per_expert, 0)


def _moe_select(afft, batch, seq, cap):
    n_exp = afft.shape[0]
    assert seq // MOE_TILE + 1 <= MOE_TAB
    return pl.pallas_call(
        functools.partial(_moe_select_kernel, cap),
        grid=(batch,),
        in_specs=[pl.BlockSpec((n_exp, seq), lambda b: (0, b))],
        out_specs=[
            pl.BlockSpec((n_exp, None, 1, cap), lambda b: (0, b, 0, 0)),
            pl.BlockSpec((n_exp * cap, LANES), lambda b: (b, 0)),
            pl.BlockSpec((n_exp, MOE_TAB), lambda b: (b, 0)),
        ],
        out_shape=[
            jax.ShapeDtypeStruct((n_exp, batch, 1, cap), I32),
            jax.ShapeDtypeStruct((batch * n_exp * cap, LANES), I32),
            jax.ShapeDtypeStruct((batch * n_exp, MOE_TAB), I32),
        ],
        scratch_shapes=[pltpu.VMEM((n_exp, seq), F32)],
        compiler_params=_cparams("arbitrary"),
        name="moe_select",
    )(afft)


MOE_PAIR = 2


MOE_CONV_ROWS = 64


def _moe_expert_kernel(seq, cap, cur, nxt, x_hbm, w1_ref, w3_ref, w2_ref, y_ref,
                       xs, xb, gate, acc, w1b, w3b, w2b, sem):
    e, h, f = pl.program_id(0), pl.program_id(1), pl.program_id(2)
    n_exp, n_half, nf = pl.num_programs(0), pl.num_programs(1), pl.num_programs(2)
    rows = MOE_PAIR * cap
    d = xb.shape[1]
    step = e * n_half + h
    last_step = n_exp * n_half - 1

    def start_row(idx_ref, half, s, pos):
        src = (half * MOE_PAIR + s) * seq + idx_ref[s, 0, pos]
        pltpu.make_async_copy(x_hbm.at[pl.ds(src, 1)], xs.at[pl.ds(s * cap + pos, 1)], sem.at[0]).start()

    def wait_rows():
        pltpu.make_async_copy(x_hbm.at[pl.ds(0, rows)], xs, sem.at[0]).wait()

    @pl.when((step == 0) & (f == 0))
    def _():
        for s in range(MOE_PAIR):
            lax.fori_loop(0, cap, lambda p, c: (start_row(cur, h, s, p), c)[1], 0)

    @pl.when(f == 0)
    def _():
        wait_rows()
        lane = lax.broadcasted_iota(I32, (MOE_CONV_ROWS, LANES), 1)
        for r in range(0, rows, MOE_CONV_ROWS):
            x = xs[r:r + MOE_CONV_ROWS, :]
            xb[r:r + MOE_CONV_ROWS, :] = x[:, :d].astype(BF16)
            gate[r:r + MOE_CONV_ROWS, :] = jnp.sum(jnp.where(lane == e, x[:, d:], 0.0), axis=1, keepdims=True)
        acc[...] = jnp.zeros_like(acc)

    per = rows // nf
    assert cap % per == 0
    nxt_h = jnp.minimum(step + 1, last_step) % n_half
    nxt_s = f // (cap // per)
    nxt_pos0 = (f % (cap // per)) * per
    for j in range(per):
        start_row(nxt, nxt_h, nxt_s, nxt_pos0 + j)

    for r in range(0, d, DOT_TM):
        w1b[r:r + DOT_TM, :] = w1_ref[r:r + DOT_TM, :].astype(BF16)
        w3b[r:r + DOT_TM, :] = w3_ref[r:r + DOT_TM, :].astype(BF16)
    for c in range(0, d, DOT_TN):
        w2b[:, c:c + DOT_TN] = w2_ref[:, c:c + DOT_TN].astype(BF16)

    for r in range(0, rows, DOT_TM):
        rs = slice(r, r + DOT_TM)
        xv = xb[rs, :]
        h1 = jnp.dot(xv, w1b[...], preferred_element_type=F32)
        h3 = jnp.dot(xv, w3b[...], preferred_element_type=F32)
        hh = (h1 / (1.0 + jnp.exp(-h1)) * h3).astype(BF16)
        for c in range(0, d, DOT_TN):
            acc[rs, c:c + DOT_TN] += jnp.dot(hh, w2b[:, c:c + DOT_TN], preferred_element_type=F32)

    @pl.when(f == nf - 1)
    def _():
        for r in range(0, rows, DOT_TM):
            y_ref[r:r + DOT_TM, :] = (acc[r:r + DOT_TM, :] * gate[r:r + DOT_TM, :]).astype(y_ref.dtype)

        @pl.when(step == last_step)
        def _():
            wait_rows()


def _moe_experts(x1e, idx, w1, w3, w2, layer, batch, seq, cap, tf):
    _, n_exp, d, ff = w1.shape
    n_half = batch // MOE_PAIR
    rows = MOE_PAIR * cap
    width = x1e.shape[1]

    def nxt(e, h):
        s = jnp.minimum(e * n_half + h + 1, n_exp * n_half - 1)
        return s // n_half, s % n_half

    idx_spec = lambda fn: pl.BlockSpec((None, MOE_PAIR, 1, cap), fn, memory_space=pltpu.SMEM)
    return pl.pallas_call(
        functools.partial(_moe_expert_kernel, seq, cap),
        grid=(n_exp, n_half, ff // tf),
        in_specs=[
            idx_spec(lambda e, h, f: (e, h, 0, 0)),
            idx_spec(lambda e, h, f: nxt(e, h) + (0, 0)),
            pl.BlockSpec(memory_space=pl.ANY),
            pl.BlockSpec((None, None, d, tf), lambda e, h, f: (layer, e, 0, f)),
            pl.BlockSpec((None, None, d, tf), lambda e, h, f: (layer, e, 0, f)),
            pl.BlockSpec((None, None, tf, d), lambda e, h, f: (layer, e, f, 0)),
        ],
        out_specs=pl.BlockSpec((rows, d), lambda e, h, f: (e * n_half + h, 0)),
        out_shape=jax.ShapeDtypeStruct((n_exp * batch * cap, d), BF16),
        scratch_shapes=[
            pltpu.VMEM((rows, width), F32),
            pltpu.VMEM((rows, d), BF16),
            pltpu.VMEM((rows, 1), F32),
            pltpu.VMEM((rows, d), F32),
            pltpu.VMEM((d, tf), BF16),
            pltpu.VMEM((d, tf), BF16),
            pltpu.VMEM((tf, d), BF16),
            pltpu.SemaphoreType.DMA((1,)),
        ],
        compiler_params=_cparams("arbitrary", "arbitrary", "arbitrary"),
        name="moe_experts",
    )(idx, idx, x1e, w1, w3, w2)


def _moe_combine_kernel(alpha, n_exp, batch, cap, tab_ref, y_hbm, tok_hbm, x_ref, g_ref, b_ref, o_ref,
                        ybuf, tokbuf, acc, nwin_ref, sem):
    b, i = pl.program_id(0), pl.program_id(1)
    n_tiles = pl.num_programs(1)
    tm = x_ref.shape[0]
    step = b * n_tiles + i
    slot = step % 2

    def window_copies(src_y, src_t, buf, off):
        dst = pl.ds(pl.multiple_of(off * MOE_WIN, MOE_WIN), MOE_WIN)
        return (pltpu.make_async_copy(y_hbm.at[pl.ds(pl.multiple_of(src_y, MOE_WIN), MOE_WIN)],
                                      ybuf.at[buf, dst], sem.at[buf, 0]),
                pltpu.make_async_copy(tok_hbm.at[pl.ds(pl.multiple_of(src_t, MOE_WIN), MOE_WIN)],
                                      tokbuf.at[buf, dst], sem.at[buf, 1]))

    def fetch_tile(bb, ii, buf):
        def per_expert(e, off):
            base = (bb * n_exp + e) * MOE_TAB + ii
            s0, s1 = tab_ref[base], tab_ref[base + 1]
            w0 = s0 // MOE_WIN
            nw = jnp.where(s1 > s0, (s1 + MOE_WIN - 1) // MOE_WIN - w0, 0)

            def per_window(w, off2):
                for cp in window_copies((e * batch + bb) * cap + (w0 + w) * MOE_WIN,
                                        (bb * n_exp + e) * cap + (w0 + w) * MOE_WIN, buf, off2):
                    cp.start()
                return off2 + 1

            return lax.fori_loop(0, nw, per_window, off)

        nwin_ref[buf] = lax.fori_loop(0, n_exp, per_expert, 0)

    @pl.when(step == 0)
    def _():
        ybuf[...] = jnp.zeros_like(ybuf)
        tokbuf[...] = jnp.zeros_like(tokbuf)
        fetch_tile(b, i, slot)

    nwin = nwin_ref[slot]

    def wait_window(w, c):
        for cp in window_copies(0, 0, slot, w):
            cp.wait()
        return c

    lax.fori_loop(0, nwin, wait_window, 0)

    @pl.when(step + 1 < pl.num_programs(0) * n_tiles)
    def _():
        nxt = step + 1
        fetch_tile(nxt // n_tiles, nxt % n_tiles, 1 - slot)

    nrows = nwin * MOE_WIN
    acc[...] = jnp.zeros_like(acc)
    sub = lax.broadcasted_iota(I32, (MOE_TILE, LANES), 0)
    lane = lax.broadcasted_iota(I32, (MOE_TILE, LANES), 1)

    def chunk(c, carry):
        rows = pl.ds(pl.multiple_of(c * MOE_TILE, MOE_TILE), MOE_TILE)
        t_local = tokbuf[slot, rows, :] - i * tm
        valid = (sub + c * MOE_TILE) < nrows
        halves = [jnp.where(valid & (t_local == lane + k * LANES), 1.0, 0.0) for k in range(tm // LANES)]
        onehot = jnp.transpose(jnp.concatenate(halves, axis=1)).astype(BF16)
        for cc in range(0, acc.shape[1], DOT_TN):
            acc[:, cc:cc + DOT_TN] += jnp.dot(onehot, ybuf[slot, rows, cc:cc + DOT_TN], preferred_element_type=F32)
        return carry

    lax.fori_loop(0, (nrows + MOE_TILE - 1) // MOE_TILE, chunk, 0)
    o_ref[...] = _layer_norm_rows(alpha * x_ref[...] + acc[...], g_ref[...], b_ref[...])


def _moe_combine(x1e, y, tok, tab, g, b, alpha, batch, seq, cap, n_exp):
    t = x1e.shape[0]
    d = y.shape[1]
    tm = MOE_TILE
    n_tiles = seq // tm
    max_rows = n_exp * (tm + MOE_WIN)
    max_rows = (max_rows + MOE_TILE - 1) // MOE_TILE * MOE_TILE
    return pl.pallas_call(
        functools.partial(_moe_combine_kernel, alpha, n_exp, batch, cap),
        grid_spec=pltpu.PrefetchScalarGridSpec(
            num_scalar_prefetch=1,
            grid=(batch, n_tiles),
            in_specs=[
                pl.BlockSpec(memory_space=pl.ANY),
                pl.BlockSpec(memory_space=pl.ANY),
                pl.BlockSpec((tm, d), lambda bb, i, tab_r: (bb * n_tiles + i, 0)),
                pl.BlockSpec((1, d), lambda bb, i, tab_r: (0, 0)),
                pl.BlockSpec((1, d), lambda bb, i, tab_r: (0, 0)),
            ],
            out_specs=pl.BlockSpec((tm, d), lambda bb, i, tab_r: (bb * n_tiles + i, 0)),
            scratch_shapes=[
                pltpu.VMEM((2, max_rows, d), BF16),
                pltpu.VMEM((2, max_rows, LANES), I32),
                pltpu.VMEM((tm, d), F32),
                pltpu.SMEM((2,), I32),
                pltpu.SemaphoreType.DMA((2, 2)),
            ],
        ),
        out_shape=jax.ShapeDtypeStruct((t, d), F32),
        compiler_params=_cparams("arbitrary", "arbitrary"),
        name="moe_combine",
    )(tab.reshape(-1), y, tok, x1e, g, b)


def _moe_block(x1e, afft, g, b, w1, w3, w2, layer, alpha, batch, seq):
    n_exp = w1.shape[1]
    cap = EC_CAPACITY_FACTOR * seq // n_exp
    idx, tok, tab = _moe_select(afft, batch, seq, cap)
    y = _moe_experts(x1e, idx, w1, w3, w2, layer, batch, seq, cap, tf=256)
    return _moe_combine(x1e, y, tok, tab, g, b, alpha, batch, seq, cap, n_exp)


def kernel(x, mix_w_in, mix_w_out, hy_conv_w, hy_conv_b, hy_ffn_w1, hy_ffn_b1, hy_ffn_w2, hy_ffn_b2, hy_ffn_w3,
           hy_ffn_b3, hy_ffn_w4, hy_sin_freq, hy_bias, pool_w, pool_scale, ln_mix_g, ln_mix_b, ln_ffn_g, ln_ffn_b,
           router_w, exp_w1, exp_w3, exp_w2):
    batch, seq, d = x.shape
    depth = ln_mix_g.shape[0]
    alpha = (2 * depth) ** 0.25
    n_exp = router_w.shape[2]
    xt = x.reshape(batch * seq, d)
    row = lambda v: v[None, :]
    for layer in range(depth):
        i = layer // 2
        rw_pad = jnp.pad(router_w[layer], ((0, 0), (0, LANES - n_exp)))
        if layer % 2 == 0:
            proj = _matmul([xt], mix_w_in[i], BF16, 1024, 1024)
            attn = _dilated_attention(proj, batch, seq)
            hcat = _hyena_filters(seq, hy_ffn_w1[i], hy_ffn_b1[i], hy_ffn_w2[i], hy_ffn_b2[i], hy_ffn_w3[i],
                                  hy_ffn_b3[i], hy_ffn_w4[i], hy_sin_freq[i])
            z, x0c = _hyena_pre(proj, hy_conv_w[i], hy_conv_b[i], seq)
            cosm, nsinm = _dft_matrices(seq)
            k_re, k_im = _dft_filter(cosm, nsinm, hcat, 2048)
            p_re, p_im = _dft_fwd(cosm, nsinm, z, k_re, k_im, batch, 2048)
            hyena = _dft_inv(cosm, nsinm, p_re, p_im, z, x0c, hy_bias[i], batch, 1024, 2048)
            mixed = _matmul([attn, hyena], mix_w_out[i], F32, 1024, 1024)
            x1e, afft = _ln_router(xt, mixed, row(ln_mix_g[layer]), row(ln_mix_b[layer]), rw_pad, alpha, n_exp)
        else:
            x1e, afft = _pool_ln_router(xt, pool_w[i], row(pool_scale[i]), row(ln_mix_g[layer]), row(ln_mix_b[layer]),
                                        rw_pad, alpha, n_exp, seq)
        xt = _moe_block(x1e, afft, row(ln_ffn_g[layer]), row(ln_ffn_b[layer]), exp_w1, exp_w3, exp_w2, layer,
                        alpha, batch, seq)
    return xt.reshape(batch, seq, d)
```

```python
import functools
import math

import numpy as np
import jax
import jax.numpy as jnp
from jax import lax
from jax.experimental import pallas as pl
from jax.experimental.pallas import tpu as pltpu

F32 = jnp.float32
BF16 = jnp.bfloat16
I32 = jnp.int32

LANES = 128
SUBLANES = 8
VMEM_LIMIT_BYTES = 56 * 1024 * 1024

ATTN_HEADS = 8
HEAD_DIM = 128
DILATED_BRANCHES = ((128, 1), (512, 4), (2048, 16))
POOL_WINDOWS = (2, 4, 8, 16)
N_EXPERTS = 16
EC_CAPACITY_FACTOR = 2
HYENA_POS_DIM = 33
HYENA_DECAY_FAST = 0.3
HYENA_DECAY_SLOW = 1.5
HYENA_DECAY_TARGET = 1e-2
LN_EPS = 1e-5
NEG_BIG = -1e30


def _cparams(*sem):
    return pltpu.CompilerParams(dimension_semantics=sem, vmem_limit_bytes=VMEM_LIMIT_BYTES)


DOT_TM = 256
DOT_TN = 512


def _dot_tiles(a_ref, b_ref, emit, cast=False):
    m, n = a_ref.shape[0], b_ref.shape[1]
    tm, tn = min(DOT_TM, m), min(DOT_TN, n)
    for r in range(0, m, tm):
        a = a_ref[r:r + tm, :]
        a = a.astype(BF16) if cast else a
        for c in range(0, n, tn):
            emit(slice(r, r + tm), slice(c, c + tn), jnp.dot(a, b_ref[:, c:c + tn], preferred_element_type=F32))


def _mm_kernel(n_in, *refs):
    x_refs, w_ref, o_ref, wb_ref = refs[:n_in], refs[n_in], refs[n_in + 1], refs[n_in + 2]

    @pl.when(pl.program_id(1) == 0)
    def _():
        for r in range(0, w_ref.shape[0], DOT_TM):
            wb_ref[r:r + DOT_TM, :] = w_ref[r:r + DOT_TM, :].astype(BF16)

    tm, tn = x_refs[0].shape[0], o_ref.shape[1]
    for r in range(0, tm, DOT_TM):
        xs = [x_ref[r:r + DOT_TM, :].astype(BF16) for x_ref in x_refs]
        for c in range(0, tn, DOT_TN):
            k0, v = 0, None
            for xv in xs:
                part = jnp.dot(xv, wb_ref[k0:k0 + xv.shape[1], c:c + DOT_TN], preferred_element_type=F32)
                v = part if v is None else v + part
                k0 += xv.shape[1]
            o_ref[r:r + DOT_TM, c:c + DOT_TN] = v.astype(o_ref.dtype)


def _matmul(xs, w, out_dtype, tm, tn):
    m = xs[0].shape[0]
    k, n = w.shape
    assert sum(x.shape[1] for x in xs) == k
    return pl.pallas_call(
        functools.partial(_mm_kernel, len(xs)),
        grid=(n // tn, m // tm),
        in_specs=[pl.BlockSpec((tm, x.shape[1]), lambda j, i: (i, 0)) for x in xs]
        + [pl.BlockSpec((k, tn), lambda j, i: (0, j))],
        out_specs=pl.BlockSpec((tm, tn), lambda j, i: (i, j)),
        out_shape=jax.ShapeDtypeStruct((m, n), out_dtype),
        scratch_shapes=[pltpu.VMEM((k, tn), BF16)],
        compiler_params=_cparams("arbitrary", "arbitrary"),
        name="matmul",
    )(*xs, w)


def _layer_norm_rows(v, g, b):
    mu = jnp.mean(v, axis=-1, keepdims=True)
    c = v - mu
    var = jnp.mean(c * c, axis=-1, keepdims=True)
    return c * lax.rsqrt(var + LN_EPS) * g + b


def _bf16_pieces(v):
    hi = v.astype(BF16)
    return hi, (v - hi.astype(F32)).astype(BF16)


def _router_affinity(x1, rw_ref, n_experts):
    xh, xl = _bf16_pieces(x1)
    wh, wl = _bf16_pieces(rw_ref[...])
    dot = functools.partial(jnp.dot, preferred_element_type=F32)
    logits = dot(xh, wh) + (dot(xh, wl) + dot(xl, wh))
    lane = lax.broadcasted_iota(I32, logits.shape, 1)
    logits = jnp.where(lane < n_experts, logits, NEG_BIG)
    mx = jnp.max(logits, axis=-1, keepdims=True)
    e = jnp.exp(logits - mx)
    return e / jnp.sum(e, axis=-1, keepdims=True)


def _ln_router_rows(alpha, n_experts, rows, x_ref, mix_ref, g_ref, b_ref, rw_ref, xe_ref, afft_ref):
    d = x_ref.shape[1]
    x1 = _layer_norm_rows(alpha * x_ref[rows, :] + mix_ref[rows, :], g_ref[...], b_ref[...])
    aff = _router_affinity(x1, rw_ref, n_experts)
    xe_ref[rows, :d] = x1
    xe_ref[rows, d:] = aff
    afft_ref[:, rows] = jnp.transpose(aff)[:n_experts, :]


def _outproj_ln_router_kernel(alpha, n_experts, a_ref, h_ref, w_ref, x_ref, g_ref, b_ref, rw_ref, xe_ref, afft_ref,
                              wb_ref, mix_ref):
    @pl.when(pl.program_id(0) == 0)
    def _():
        for r in range(0, w_ref.shape[0], DOT_TM):
            wb_ref[r:r + DOT_TM, :] = w_ref[r:r + DOT_TM, :].astype(BF16)

    tm, ka = a_ref.shape
    d = x_ref.shape[1]
    dot = functools.partial(jnp.dot, preferred_element_type=F32)
    for r in range(0, tm, DOT_TM):
        rows = slice(r, r + DOT_TM)
        av, hv = a_ref[rows, :], h_ref[rows, :]
        for c in range(0, d, DOT_TN):
            cols = slice(c, c + DOT_TN)
            mix_ref[rows, cols] = dot(av, wb_ref[:ka, cols]) + dot(hv, wb_ref[ka:, cols])
        _ln_router_rows(alpha, n_experts, rows, x_ref, mix_ref, g_ref, b_ref, rw_ref, xe_ref, afft_ref)


def _outproj_ln_router(attn, hyena, w_out, x, g, b, rw_pad, alpha, n_experts, tm=512):
    t, d = x.shape
    const = lambda a: pl.BlockSpec(a.shape, lambda i: (0,) * a.ndim)
    return pl.pallas_call(
        functools.partial(_outproj_ln_router_kernel, alpha, n_experts),
        grid=(t // tm,),
        in_specs=[
            pl.BlockSpec((tm, attn.shape[1]), lambda i: (i, 0)),
            pl.BlockSpec((tm, hyena.shape[1]), lambda i: (i, 0)),
            pl.BlockSpec(w_out.shape, lambda i: (0, 0), pipeline_mode=pl.Buffered(1)),
            pl.BlockSpec((tm, d), lambda i: (i, 0)),
            const(g), const(b), const(rw_pad),
        ],
        out_specs=[pl.BlockSpec((tm, d + LANES), lambda i: (i, 0)), pl.BlockSpec((n_experts, tm), lambda i: (0, i))],
        out_shape=[jax.ShapeDtypeStruct((t, d + LANES), F32), jax.ShapeDtypeStruct((n_experts, t), F32)],
        scratch_shapes=[pltpu.VMEM(w_out.shape, BF16), pltpu.VMEM((tm, d), F32)],
        compiler_params=_cparams("arbitrary"),
        name="outproj_ln_router",
    )(attn, hyena, w_out, x, g, b, rw_pad)


ATTN_QBLK = 128
ATTN_UNROLL = 16
ATTN_HALF = 64
ATTN_WINDOW_LEADS = (0, ATTN_HALF, 2 * ATTN_HALF)


def _attn_branch(slope, dil, lc, q_cm, k_cm, v_cm, o_cm, l_cm, seq, bias_ref):
    win = min(ATTN_QBLK + 2 * ATTN_HALF, lc)
    blocks_per_class = lc // ATTN_QBLK
    scale = 1.0 / math.sqrt(HEAD_DIM)
    diff = lax.broadcasted_iota(I32, (ATTN_QBLK, win), 1) - lax.broadcasted_iota(I32, (ATTN_QBLK, win), 0)
    for case in range(len(ATTN_WINDOW_LEADS)):
        rel = jnp.abs(diff - ATTN_WINDOW_LEADS[case])
        bias_ref[case, :, :win] = jnp.where(rel <= ATTN_HALF, -(slope * dil) * rel.astype(F32), NEG_BIG)

    def body(blk, carry):
        cls = blk // blocks_per_class
        q0 = (blk % blocks_per_class) * ATTN_QBLK
        w0 = jnp.clip(q0 - ATTN_HALF, 0, lc - win)
        qs = pl.multiple_of(cls * lc + q0, ATTN_HALF)
        ks = pl.multiple_of(cls * lc + w0, ATTN_HALF)
        q = q_cm[pl.ds(qs, ATTN_QBLK), :]
        k = k_cm[pl.ds(ks, win), :]
        v = v_cm[pl.ds(ks, win), :]
        s = lax.dot_general(q, k, (((1,), (1,)), ((), ())), preferred_element_type=F32) * scale
        bias = bias_ref[(q0 - w0) // ATTN_HALF, :, :win]
        s = jnp.where(bias > 0.5 * NEG_BIG, s + bias, NEG_BIG)
        m = jnp.max(s, axis=-1, keepdims=True)
        p = jnp.exp(s - m)
        l = jnp.sum(p, axis=-1, keepdims=True)
        o = jnp.dot(p.astype(BF16), v, preferred_element_type=F32) / l
        o_cm[pl.ds(qs, ATTN_QBLK), :] = o
        l_cm[pl.ds(qs, ATTN_QBLK), :] = jnp.broadcast_to(m + jnp.log(l), (ATTN_QBLK, HEAD_DIM))
        return carry

    lax.fori_loop(0, seq // ATTN_QBLK, body, 0, unroll=ATTN_UNROLL)


def _attn_kernel(slopes_ref, q_ref, k_ref, v_ref, out_ref, qf, kf, vf, qc, kc, vc, ocm, lcm, onat, lnat, bias_ref):
    seq = q_ref.shape[0]
    slope = slopes_ref[pl.program_id(1)]
    qf[...] = q_ref[...].astype(F32)
    kf[...] = k_ref[...].astype(F32)
    vf[...] = v_ref[...].astype(F32)
    for bi, (window, dil) in enumerate(DILATED_BRANCHES):
        assert window // (2 * dil) == ATTN_HALF
        lc = seq // dil
        if dil == 1:
            _attn_branch(slope, dil, lc, q_ref, k_ref, v_ref, onat.at[bi], lnat.at[bi], seq, bias_ref)
            continue
        for r in range(dil):
            qc[r * lc:(r + 1) * lc, :] = qf[pl.ds(r, lc, stride=dil), :].astype(BF16)
            kc[r * lc:(r + 1) * lc, :] = kf[pl.ds(r, lc, stride=dil), :].astype(BF16)
            vc[r * lc:(r + 1) * lc, :] = vf[pl.ds(r, lc, stride=dil), :].astype(BF16)
        _attn_branch(slope, dil, lc, qc, kc, vc, ocm, lcm, seq, bias_ref)
        for r in range(dil):
            onat[bi, pl.ds(r, lc, stride=dil), :] = ocm[r * lc:(r + 1) * lc, :]
            lnat[bi, pl.ds(r, lc, stride=dil), :] = lcm[r * lc:(r + 1) * lc, :]
    lses = [lnat[bi] for bi in range(len(DILATED_BRANCHES))]
    mx = functools.reduce(jnp.maximum, lses)
    ws = [jnp.exp(l - mx) for l in lses]
    num = sum(w * onat[bi] for bi, w in enumerate(ws))
    out_ref[...] = (num / sum(ws)).astype(out_ref.dtype)


def _dilated_attention(proj, batch, seq):
    t = proj.shape[0]
    slopes = jnp.asarray(2.0 ** (-(8.0 / ATTN_HEADS) * np.arange(1, ATTN_HEADS + 1)), F32)
    blk = (seq, HEAD_DIM)
    nb = len(DILATED_BRANCHES)
    return pl.pallas_call(
        _attn_kernel,
        grid_spec=pltpu.PrefetchScalarGridSpec(
            num_scalar_prefetch=1,
            grid=(batch, ATTN_HEADS),
            in_specs=[
                pl.BlockSpec(blk, lambda b, h, s: (b, h)),
                pl.BlockSpec(blk, lambda b, h, s: (b, ATTN_HEADS + h)),
                pl.BlockSpec(blk, lambda b, h, s: (b, 2 * ATTN_HEADS + h)),
            ],
            out_specs=pl.BlockSpec(blk, lambda b, h, s: (b, h)),
            scratch_shapes=[pltpu.VMEM(blk, F32)] * 3 + [pltpu.VMEM(blk, BF16)] * 3 + [pltpu.VMEM(blk, F32)] * 2
            + [pltpu.VMEM((nb,) + blk, F32)] * 2
            + [pltpu.VMEM((len(ATTN_WINDOW_LEADS), ATTN_QBLK, ATTN_QBLK + 2 * ATTN_HALF), F32)],
        ),
        out_shape=jax.ShapeDtypeStruct((t, ATTN_HEADS * HEAD_DIM), BF16),
        compiler_params=_cparams("arbitrary", "arbitrary"),
        name="dilated_attention",
    )(slopes, proj, proj, proj)


HYENA_ORDER_PAD = LANES


def _filter_kernel(z_ref, w1, b1, w2, b2, w3, b3, w4, fr, absdelta_ref, h_ref):
    hi = lax.Precision.HIGHEST
    f = fr[...]
    z = z_ref[...]
    h = jnp.sin(f * (jnp.dot(z, w1[...], preferred_element_type=F32, precision=hi) + b1[...]))
    h = jnp.sin(f * (jnp.dot(h, w2[...], preferred_element_type=F32, precision=hi) + b2[...]))
    h = jnp.sin(f * (jnp.dot(h, w3[...], preferred_element_type=F32, precision=hi) + b3[...]))
    h = jnp.dot(h, w4[...], preferred_element_type=F32, precision=hi)
    t = z[:, 0:1]
    h = h * jnp.exp(-t * absdelta_ref[...])
    c = h.shape[1] // 2
    row = lax.broadcasted_iota(I32, h.shape, 0) + pl.program_id(0) * h.shape[0]
    col = lax.broadcasted_iota(I32, h.shape, 1)
    h_ref[...] = jnp.where((row == 0) & (col >= c), 0.0, h).astype(h_ref.dtype)


def _position_features(seq):
    t = np.linspace(0.0, 1.0, seq)[:, None]
    bands = (HYENA_POS_DIM - 1) // 2
    w_ang = 2.0 * np.pi * np.arange(seq)[:, None] / seq
    f = np.linspace(1e-4, bands - 1, bands)[None, :]
    z = np.concatenate([t, np.cos(f * w_ang), -np.sin(f * w_ang)], axis=-1)
    return np.pad(z, ((0, 0), (0, HYENA_ORDER_PAD - z.shape[1]))).astype(np.float32)


def _pad2(a, rows, cols):
    return jnp.pad(a, ((0, rows - a.shape[0]), (0, cols - a.shape[1])))


def _hyena_filters(seq, w1, b1, w2, b2, w3, b3, w4, freq, tl=512):
    p = HYENA_ORDER_PAD
    c2 = w4.shape[1]
    c = c2 // 2
    max_decay = math.log(HYENA_DECAY_TARGET) / HYENA_DECAY_FAST
    min_decay = math.log(HYENA_DECAY_TARGET) / HYENA_DECAY_SLOW
    absdelta = np.abs(np.linspace(min_decay, max_decay, c))
    absdelta = jnp.asarray(np.concatenate([absdelta, absdelta])[None, :], F32)
    args = (
        jnp.asarray(_position_features(seq)),
        _pad2(w1, p, p), _pad2(b1[None, :], 1, p), _pad2(w2, p, p), _pad2(b2[None, :], 1, p),
        _pad2(w3, p, p), _pad2(b3[None, :], 1, p), _pad2(w4, p, c2), _pad2(freq[None, :], 1, p), absdelta,
    )
    full = lambda a: pl.BlockSpec(a.shape, lambda i: (0, 0))
    return pl.pallas_call(
        _filter_kernel,
        grid=(seq // tl,),
        in_specs=[pl.BlockSpec((tl, p), lambda i: (i, 0))] + [full(a) for a in args[1:]],
        out_specs=pl.BlockSpec((tl, c2), lambda i: (i, 0)),
        out_shape=jax.ShapeDtypeStruct((seq, c2), BF16),
        compiler_params=_cparams("arbitrary"),
        name="hyena_filters",
    )(*args)


def _shift_rows(u, edge_row, down):
    n = u.shape[0]
    row = lax.broadcasted_iota(I32, u.shape, 0)
    if down:
        return jnp.where(row == 0, edge_row, pltpu.roll(u, 1, 0))
    return jnp.where(row == n - 1, edge_row, pltpu.roll(u, n - 1, 0))


def _hyena_pre_kernel(tiles_per_seq, cur_ref, prev_ref, next_ref, w_ref, b_ref, z_ref, x0_ref):
    i = pl.program_id(0) % tiles_per_seq
    u = cur_ref[...].astype(F32)
    prev_row = jnp.where(i == 0, 0.0, prev_ref[SUBLANES - 1:SUBLANES, :].astype(F32))
    next_row = jnp.where(i == tiles_per_seq - 1, 0.0, next_ref[0:1, :].astype(F32))
    y = (w_ref[0:1, :] * _shift_rows(u, prev_row, True) + w_ref[1:2, :] * u
         + w_ref[2:3, :] * _shift_rows(u, next_row, False) + b_ref[...])
    c = y.shape[1] // 3
    x0_ref[...] = y[:, :c].astype(x0_ref.dtype)
    z_ref[...] = (y[:, 2 * c:] * y[:, c:2 * c]).astype(z_ref.dtype)


def _hyena_pre(proj, conv_w, conv_b, seq, ts=256):
    t, width = proj.shape
    c3 = conv_w.shape[1]
    assert width == 2 * c3
    c = c3 // 3
    tiles_per_seq = seq // ts
    r8 = ts // SUBLANES
    last8 = t // SUBLANES - 1
    return pl.pallas_call(
        functools.partial(_hyena_pre_kernel, tiles_per_seq),
        grid=(t // ts,),
        in_specs=[
            pl.BlockSpec((ts, c3), lambda i: (i, 1)),
            pl.BlockSpec((SUBLANES, c3), lambda i: (jnp.maximum(i * r8 - 1, 0), 1)),
            pl.BlockSpec((SUBLANES, c3), lambda i: (jnp.minimum((i + 1) * r8, last8), 1)),
            pl.BlockSpec((3, c3), lambda i: (0, 0)),
            pl.BlockSpec((1, c3), lambda i: (0, 0)),
        ],
        out_specs=[pl.BlockSpec((ts, c), lambda i: (i, 0))] * 2,
        out_shape=[jax.ShapeDtypeStruct((t, c), BF16)] * 2,
        compiler_params=_cparams("arbitrary"),
        name="hyena_short_conv",
    )(proj, proj, proj, conv_w, conv_b[None, :])


DFT_FBLK = 512
DFT_SPLIT = 64
DFT_GEN_ROWS = 256
DFT_EPI_ROWS = 64
BF16_ROWS = 2 * SUBLANES


def _dft_tables(seq):
    n = 2 * seq
    k = np.arange(seq)[:, None]
    a = np.arange(seq // DFT_SPLIT)[None, :]
    b = np.arange(DFT_SPLIT)[None, :]
    ang1 = 2.0 * np.pi * ((k * a * DFT_SPLIT) % n) / n
    ang2 = 2.0 * np.pi * ((k * b) % n) / n
    return tuple(jnp.asarray(f(a_), F32) for a_ in (ang1, ang2) for f in (np.cos, np.sin))


def _dft_gen_kernel(c1_ref, s1_ref, c2_ref, s2_ref, cos_ref, nsin_ref):
    rows, na = c1_ref.shape
    seq = cos_ref.shape[1]
    reps = DOT_TN // DFT_SPLIT
    c2 = jnp.concatenate([c2_ref[...]] * reps, axis=1)
    s2 = jnp.concatenate([s2_ref[...]] * reps, axis=1)

    c1p, s1p = _bf16_pieces(c1_ref[...]), _bf16_pieces(s1_ref[...])
    for c in range(0, seq, DOT_TN):
        n_of_lane = lax.broadcasted_iota(I32, (na, DOT_TN), 1) + c
        expand = jnp.where(n_of_lane // DFT_SPLIT == lax.broadcasted_iota(I32, (na, DOT_TN), 0), 1.0, 0.0).astype(BF16)
        a, b = (jnp.dot(hi, expand, preferred_element_type=F32) + jnp.dot(lo, expand, preferred_element_type=F32)
                for hi, lo in (c1p, s1p))
        cos_ref[:, c:c + DOT_TN] = (a * c2 - b * s2).astype(cos_ref.dtype)
        nsin_ref[:, c:c + DOT_TN] = (-(b * c2 + a * s2)).astype(nsin_ref.dtype)


def _dft_matrices(seq):
    tables = _dft_tables(seq)
    tr = DFT_GEN_ROWS
    return pl.pallas_call(
        _dft_gen_kernel,
        grid=(seq // tr,),
        in_specs=[pl.BlockSpec((tr, t.shape[1]), lambda i: (i, 0)) for t in tables],
        out_specs=[pl.BlockSpec((tr, seq), lambda i: (i, 0))] * 2,
        out_shape=[jax.ShapeDtypeStruct((seq, seq), BF16)] * 2,
        compiler_params=_cparams("arbitrary"),
        name="dft_matrices",
    )(*tables)


def _nyquist_rows(tk):
    lane = lax.broadcasted_iota(I32, (SUBLANES, tk), 1)
    return jnp.where(lane % 2 == 0, 1.0, -1.0).astype(BF16)


def _dft_accumulate(cos_ref, nsin_ref, u_ref, re_ref, im_ref, ny_ref, kk, first_block):
    @pl.when(kk == 0)
    def _():
        re_ref[...] = jnp.zeros_like(re_ref)
        im_ref[...] = jnp.zeros_like(im_ref)
        ny_ref[...] = jnp.zeros_like(ny_ref)

    def add_re(rows, cols, v):
        re_ref[rows, cols] += v

    def add_im(rows, cols, v):
        im_ref[rows, cols] += v

    _dot_tiles(cos_ref, u_ref, add_re)
    _dot_tiles(nsin_ref, u_ref, add_im)

    @pl.when(first_block)
    def _():
        ny_ref[...] += jnp.dot(_nyquist_rows(u_ref.shape[0]), u_ref[...], preferred_element_type=F32)


def _dft_filter_kernel(cos_ref, nsin_ref, h_ref, kre_ref, kim_ref, re_ref, im_ref, ny_ref):
    i, kk = pl.program_id(0), pl.program_id(1)
    _dft_accumulate(cos_ref, nsin_ref, h_ref, re_ref, im_ref, ny_ref, kk, i == 0)

    @pl.when(kk == pl.num_programs(1) - 1)
    def _():
        c = kre_ref.shape[1]
        kre_ref[...] = re_ref[:, :c] + re_ref[:, c:]
        kim_ref[...] = im_ref[:, :c] - im_ref[:, c:]

        @pl.when(i == 0)
        def _():
            kim_ref[0:1, :] = ny_ref[0:1, :c] + ny_ref[0:1, c:]


def _dft_filter(cosm, nsinm, hcat, tk):
    seq = cosm.shape[0]
    c2 = hcat.shape[1]
    mat = pl.BlockSpec((DFT_FBLK, tk), lambda i, kk: (i, kk))
    return pl.pallas_call(
        _dft_filter_kernel,
        grid=(seq // DFT_FBLK, seq // tk),
        in_specs=[mat, mat, pl.BlockSpec((tk, c2), lambda i, kk: (kk, 0))],
        out_specs=[pl.BlockSpec((DFT_FBLK, c2 // 2), lambda i, kk: (i, 0))] * 2,
        out_shape=[jax.ShapeDtypeStruct((seq, c2 // 2), F32)] * 2,
        scratch_shapes=[pltpu.VMEM((DFT_FBLK, c2), F32)] * 2 + [pltpu.VMEM((SUBLANES, c2), F32)],
        compiler_params=_cparams("arbitrary", "arbitrary"),
        name="dft_filter",
    )(cosm, nsinm, hcat)


def _dft_fwd_kernel(seq, cos_ref, nsin_ref, z_ref, kre_ref, kim_ref, pre_ref, pim_ref, re_ref, im_ref, ny_ref):
    i, kk = pl.program_id(1), pl.program_id(2)
    _dft_accumulate(cos_ref, nsin_ref, z_ref, re_ref, im_ref, ny_ref, kk, i == 0)

    @pl.when(kk == pl.num_programs(2) - 1)
    def _():
        inv_n = 1.0 / (2 * seq)
        for r in range(0, DFT_FBLK, DFT_EPI_ROWS):
            rs = slice(r, r + DFT_EPI_ROWS)
            ure, uim, kre, kim = re_ref[rs, :], im_ref[rs, :], kre_ref[rs, :], kim_ref[rs, :]
            pre_ref[rs, :] = ((ure * kre - uim * kim) * (2 * inv_n)).astype(pre_ref.dtype)
            pim_ref[rs, :] = ((ure * kim + uim * kre) * (2 * inv_n)).astype(pim_ref.dtype)

        @pl.when(i == 0)
        def _():
            head = slice(0, BF16_ROWS)
            ure, uim, kre, kim = re_ref[head, :], im_ref[head, :], kre_ref[head, :], kim_ref[head, :]
            first = lax.broadcasted_iota(I32, ure.shape, 0) == 0
            p_re = jnp.where(first, ure * kre * inv_n, (ure * kre - uim * kim) * (2 * inv_n))
            p_im = jnp.where(first, ny_ref[0:1, :] * kim * inv_n, (ure * kim + uim * kre) * (2 * inv_n))
            pre_ref[head, :] = p_re.astype(pre_ref.dtype)
            pim_ref[head, :] = p_im.astype(pim_ref.dtype)


def _dft_fwd(cosm, nsinm, z, kre, kim, batch, tk):
    seq = cosm.shape[0]
    c = z.shape[1]
    nfb = seq // DFT_FBLK
    nk = seq // tk
    mat = pl.BlockSpec((DFT_FBLK, tk), lambda b, i, kk: (i, kk))
    spec = pl.BlockSpec((DFT_FBLK, c), lambda b, i, kk: (i, 0))
    return pl.pallas_call(
        functools.partial(_dft_fwd_kernel, seq),
        grid=(batch, nfb, nk),
        in_specs=[mat, mat, pl.BlockSpec((tk, c), lambda b, i, kk: (b * nk + kk, 0)), spec, spec],
        out_specs=[pl.BlockSpec((DFT_FBLK, c), lambda b, i, kk: (b * nfb + i, 0))] * 2,
        out_shape=[jax.ShapeDtypeStruct((batch * seq, c), BF16)] * 2,
        scratch_shapes=[pltpu.VMEM((DFT_FBLK, c), F32)] * 2 + [pltpu.VMEM((SUBLANES, c), F32)],
        compiler_params=_cparams("arbitrary", "arbitrary", "arbitrary"),
        name="dft_fwd",
    )(cosm, nsinm, z, kre, kim)


def _dft_inv_kernel(cos_ref, nsin_ref, pre_ref, pim_ref, pny_ref, z_ref, x0_ref, bias_ref, o_ref, acc_ref):
    it, kk = pl.program_id(1), pl.program_id(2)
    tt = acc_ref.shape[0]

    @pl.when(kk == 0)
    def _():
        acc_ref[...] = jnp.zeros_like(acc_ref)

    def accumulate(rows, cols, v):
        acc_ref[rows, cols] += v

    _dot_tiles(cos_ref, pre_ref, accumulate)
    _dot_tiles(nsin_ref, pim_ref, accumulate)

    @pl.when(kk == pl.num_programs(2) - 1)
    def _():
        nyq = pny_ref[0:1, :].astype(F32)
        for r in range(0, tt, DOT_TM):
            rs = slice(r, r + DOT_TM)
            t = it * tt + r + lax.broadcasted_iota(I32, (DOT_TM, 1), 0)
            y = acc_ref[rs, :] + jnp.where(t % 2 == 0, 1.0, -1.0) * nyq
            zz = z_ref[rs, :].astype(F32)
            o_ref[rs, :] = (x0_ref[rs, :].astype(F32) * (y + zz * bias_ref[...])).astype(o_ref.dtype)


def _dft_inv(cosm, nsinm, p_re, p_im, z, x0c, hy_bias, batch, tt, tk):
    seq = cosm.shape[0]
    c = z.shape[1]
    nt = seq // tt
    nk = seq // tk
    mat = pl.BlockSpec((tt, tk), lambda b, i, kk: (i, kk))
    spec = pl.BlockSpec((tk, c), lambda b, i, kk: (b * nk + kk, 0))
    tile = pl.BlockSpec((tt, c), lambda b, i, kk: (b * nt + i, 0))
    return pl.pallas_call(
        _dft_inv_kernel,
        grid=(batch, nt, nk),
        in_specs=[mat, mat, spec, spec, pl.BlockSpec((BF16_ROWS, c), lambda b, i, kk: (b * (seq // BF16_ROWS), 0)),
                  tile, tile, pl.BlockSpec((1, c), lambda b, i, kk: (0, 0))],
        out_specs=tile,
        out_shape=jax.ShapeDtypeStruct((batch * seq, c), BF16),
        scratch_shapes=[pltpu.VMEM((tt, c), F32)],
        compiler_params=_cparams("arbitrary", "arbitrary", "arbitrary"),
        name="dft_inv",
    )(cosm, nsinm, p_re, p_im, p_im, z, x0c, hy_bias[None, :])


POOL_HALO = SUBLANES


def _pool_ln_router_kernel(alpha, n_experts, seq, x_ref, prev_ref, next_ref, pw_ref, ps_ref, g_ref, b_ref, rw_ref,
                           xe_ref, afft_ref, ext_ref, mix_ref):
    tm, d = x_ref.shape
    tiles_per_seq = seq // tm
    i = pl.program_id(0) % tiles_per_seq
    ext_ref[:POOL_HALO, :] = jnp.where(i == 0, 0.0, prev_ref[...])
    ext_ref[POOL_HALO:POOL_HALO + tm, :] = x_ref[...]
    ext_ref[POOL_HALO + tm:, :] = jnp.where(i == tiles_per_seq - 1, 0.0, next_ref[...])
    pos = i * tm + lax.broadcasted_iota(I32, (tm, 1), 0)
    group = d // len(POOL_WINDOWS)
    for gi, win in enumerate(POOL_WINDOWS):
        half = win // 2
        cols = slice(gi * group, (gi + 1) * group)
        wsum = ext_ref[POOL_HALO - half:POOL_HALO - half + tm, cols]
        for j in range(1 - half, half):
            wsum = wsum + ext_ref[POOL_HALO + j:POOL_HALO + j + tm, cols]
        count = (jnp.minimum(pos + half, seq) - jnp.maximum(pos - half, 0)).astype(F32)
        dev = wsum / count - x_ref[:, cols]
        mixed = jnp.dot(dev.astype(BF16), pw_ref[gi].astype(BF16), preferred_element_type=F32)
        mix_ref[:, cols] = mixed * ps_ref[:, cols]
    _ln_router_rows(alpha, n_experts, slice(0, tm), x_ref, mix_ref, g_ref, b_ref, rw_ref, xe_ref, afft_ref)


def _pool_ln_router(x, pool_w, pool_scale, g, b, rw_pad, alpha, n_experts, seq, tm=256):
    t, d = x.shape
    r8 = tm // SUBLANES
    last8 = t // SUBLANES - 1
    const = lambda a: pl.BlockSpec(a.shape, lambda i: (0,) * a.ndim)
    return pl.pallas_call(
        functools.partial(_pool_ln_router_kernel, alpha, n_experts, seq),
        grid=(t // tm,),
        in_specs=[
            pl.BlockSpec((tm, d), lambda i: (i, 0)),
            pl.BlockSpec((SUBLANES, d), lambda i: (jnp.maximum(i * r8 - 1, 0), 0)),
            pl.BlockSpec((SUBLANES, d), lambda i: (jnp.minimum((i + 1) * r8, last8), 0)),
            const(pool_w), const(pool_scale), const(g), const(b), const(rw_pad),
        ],
        out_specs=[pl.BlockSpec((tm, d + LANES), lambda i: (i, 0)), pl.BlockSpec((n_experts, tm), lambda i: (0, i))],
        out_shape=[jax.ShapeDtypeStruct((t, d + LANES), F32), jax.ShapeDtypeStruct((n_experts, t), F32)],
        scratch_shapes=[pltpu.VMEM((tm + 2 * POOL_HALO, d), F32), pltpu.VMEM((tm, d), F32)],
        compiler_params=_cparams("arbitrary"),
        name="pool_ln_router",
    )(x, x, x, pool_w, pool_scale, g, b, rw_pad)


MOE_TILE = 256
MOE_TAB = 32
MOE_JBLK = 128
MOE_WIN = 16


def _lane_cumsum(mask_f, tri):
    rows, n = mask_f.shape
    run = jnp.zeros((rows, 1), F32)
    parts, starts = [], []
    for c in range(n // MOE_TILE):
        starts.append(run)
        m = mask_f[:, c * MOE_TILE:(c + 1) * MOE_TILE].astype(BF16)
        cs = jnp.dot(m, tri, preferred_element_type=F32) + run
        parts.append(cs)
        run = cs[:, MOE_TILE - 1:MOE_TILE]
    starts.append(run)
    return jnp.concatenate(parts, axis=1), starts


def _moe_select_kernel(cap, afft_ref, idx_ref, tok_ref, tab_ref, csel_ref):
    n_exp, seq = afft_ref.shape
    aff = afft_ref[...]

    def search(it, thr_bits):
        cand = thr_bits | jnp.left_shift(jnp.int32(1), 30 - it)
        cnt = jnp.sum(jnp.where(aff >= pltpu.bitcast(cand, F32), 1.0, 0.0), axis=1, keepdims=True)
        return jnp.where(cnt >= cap, cand, thr_bits)

    thr = pltpu.bitcast(lax.fori_loop(0, 31, search, jnp.zeros((n_exp, 1), I32)), F32)
    gt = aff > thr
    eq = aff == thr
    need = cap - jnp.sum(jnp.where(gt, 1.0, 0.0), axis=1, keepdims=True)
    r = lax.broadcasted_iota(I32, (MOE_TILE, MOE_TILE), 0)
    c = lax.broadcasted_iota(I32, (MOE_TILE, MOE_TILE), 1)
    tri = jnp.where(r <= c, 1.0, 0.0).astype(BF16)
    ceq, _ = _lane_cumsum(jnp.where(eq, 1.0, 0.0), tri)
    sel = jnp.where(gt, 1.0, jnp.where(eq & (ceq <= need), 1.0, 0.0))
    csel, starts = _lane_cumsum(sel, tri)
    csel_ref[...] = csel
    lane = lax.broadcasted_iota(I32, (n_exp, MOE_TAB), 1)
    tab = jnp.zeros((n_exp, MOE_TAB), I32)
    for ti, s in enumerate(starts):
        tab = jnp.where(lane == ti, s.astype(I32), tab)
    tab_ref[...] = tab

    def per_expert(e, carry):
        row = csel_ref[pl.ds(e, 1), :]
        for jb in range(cap // MOE_JBLK):
            jcol = (lax.broadcasted_iota(I32, (MOE_JBLK, LANES), 0) + jb * MOE_JBLK).astype(F32)
            acc = jnp.zeros((MOE_JBLK, LANES), F32)
            for tc in range(seq // LANES):
                acc = acc + jnp.where(row[:, tc * LANES:(tc + 1) * LANES] <= jcol, 1.0, 0.0)
            tokcol = jnp.sum(acc, axis=1, keepdims=True).astype(I32)
            tokb = jnp.broadcast_to(tokcol, (MOE_JBLK, LANES))
            tok_ref[pl.ds(pl.multiple_of(e * cap + jb * MOE_JBLK, MOE_JBLK), MOE_JBLK), :] = tokb
            idx_ref[e, :, jb * MOE_JBLK:(jb + 1) * MOE_JBLK] = jnp.transpose(tokb)[0:1, :]
        return carry

    lax.fori_loop(0, n_exp, per_expert, 0)


def _moe_select(afft, batch, seq, cap):
    n_exp = afft.shape[0]
    assert seq // MOE_TILE + 1 <= MOE_TAB
    return pl.pallas_call(
        functools.partial(_moe_select_kernel, cap),
        grid=(batch,),
        in_specs=[pl.BlockSpec((n_exp, seq), lambda b: (0, b))],
        out_specs=[
            pl.BlockSpec((n_exp, None, 1, cap), lambda b: (0, b, 0, 0)),
            pl.BlockSpec((n_exp * cap, LANES), lambda b: (b, 0)),
            pl.BlockSpec((n_exp, MOE_TAB), lambda b: (b, 0)),
        ],
        out_shape=[
            jax.ShapeDtypeStruct((n_exp, batch, 1, cap), I32),
            jax.ShapeDtypeStruct((batch * n_exp * cap, LANES), I32),
            jax.ShapeDtypeStruct((batch * n_exp, MOE_TAB), I32),
        ],
        scratch_shapes=[pltpu.VMEM((n_exp, seq), F32)],
        compiler_params=_cparams("arbitrary"),
        name="moe_select",
    )(afft)


MOE_PAIR = 2


MOE_CONV_ROWS = 64
MOE_EXPERT_FF_TILE = 512
MOE_EXPERT_VMEM_LIMIT_BYTES = 61 * 1024 * 1024


def _moe_expert_kernel(seq, cap, cur, nxt, x_hbm, w1_ref, w3_ref, w2_ref, y_ref,
                       xs, xb, gate, acc, w1b, w3b, w2b, sem):
    e, h, f = pl.program_id(0), pl.program_id(1), pl.program_id(2)
    n_exp, n_half, nf = pl.num_programs(0), pl.num_programs(1), pl.num_programs(2)
    rows = MOE_PAIR * cap
    d = xb.shape[1]
    step = e * n_half + h
    last_step = n_exp * n_half - 1

    def start_row(idx_ref, half, s, pos):
        src = (half * MOE_PAIR + s) * seq + idx_ref[s, 0, pos]
        pltpu.make_async_copy(x_hbm.at[pl.ds(src, 1)], xs.at[pl.ds(s * cap + pos, 1)], sem.at[0]).start()

    def wait_rows():
        pltpu.make_async_copy(x_hbm.at[pl.ds(0, rows)], xs, sem.at[0]).wait()

    @pl.when((step == 0) & (f == 0))
    def _():
        for s in range(MOE_PAIR):
            lax.fori_loop(0, cap, lambda p, c: (start_row(cur, h, s, p), c)[1], 0)

    @pl.when(f == 0)
    def _():
        wait_rows()
        lane = lax.broadcasted_iota(I32, (MOE_CONV_ROWS, LANES), 1)
        for r in range(0, rows, MOE_CONV_ROWS):
            x = xs[r:r + MOE_CONV_ROWS, :]
            xb[r:r + MOE_CONV_ROWS, :] = x[:, :d].astype(BF16)
            gate[r:r + MOE_CONV_ROWS, :] = jnp.sum(jnp.where(lane == e, x[:, d:], 0.0), axis=1, keepdims=True)
        acc[...] = jnp.zeros_like(acc)

    per = rows // nf
    assert cap % per == 0
    nxt_h = jnp.minimum(step + 1, last_step) % n_half
    nxt_s = f // (cap // per)
    nxt_pos0 = (f % (cap // per)) * per
    for j in range(per):
        start_row(nxt, nxt_h, nxt_s, nxt_pos0 + j)

    for r in range(0, d, DOT_TM):
        w1b[r:r + DOT_TM, :] = w1_ref[r:r + DOT_TM, :].astype(BF16)
        w3b[r:r + DOT_TM, :] = w3_ref[r:r + DOT_TM, :].astype(BF16)
    for c in range(0, d, DOT_TN):
        w2b[:, c:c + DOT_TN] = w2_ref[:, c:c + DOT_TN].astype(BF16)

    for r in range(0, rows, DOT_TM):
        rs = slice(r, r + DOT_TM)
        xv = xb[rs, :]
        h1 = jnp.dot(xv, w1b[...], preferred_element_type=F32)
        h3 = jnp.dot(xv, w3b[...], preferred_element_type=F32)
        hh = (h1 / (1.0 + jnp.exp(-h1)) * h3).astype(BF16)
        for c in range(0, d, DOT_TN):
            acc[rs, c:c + DOT_TN] += jnp.dot(hh, w2b[:, c:c + DOT_TN], preferred_element_type=F32)

    @pl.when(f == nf - 1)
    def _():
        for r in range(0, rows, DOT_TM):
            y_ref[r:r + DOT_TM, :] = (acc[r:r + DOT_TM, :] * gate[r:r + DOT_TM, :]).astype(y_ref.dtype)

        @pl.when(step == last_step)
        def _():
            wait_rows()


def _moe_experts(x1e, idx, w1, w3, w2, layer, batch, seq, cap, tf):
    _, n_exp, d, ff = w1.shape
    n_half = batch // MOE_PAIR
    rows = MOE_PAIR * cap
    width = x1e.shape[1]

    def nxt(e, h):
        s = jnp.minimum(e * n_half + h + 1, n_exp * n_half - 1)
        return s // n_half, s % n_half

    idx_spec = lambda fn: pl.BlockSpec((None, MOE_PAIR, 1, cap), fn, memory_space=pltpu.SMEM)
    return pl.pallas_call(
        functools.partial(_moe_expert_kernel, seq, cap),
        grid=(n_exp, n_half, ff // tf),
        in_specs=[
            idx_spec(lambda e, h, f: (e, h, 0, 0)),
            idx_spec(lambda e, h, f: nxt(e, h) + (0, 0)),
            pl.BlockSpec(memory_space=pl.ANY),
            pl.BlockSpec((None, None, d, tf), lambda e, h, f: (layer, e, 0, f)),
            pl.BlockSpec((None, None, d, tf), lambda e, h, f: (layer, e, 0, f)),
            pl.BlockSpec((None, None, tf, d), lambda e, h, f: (layer, e, f, 0)),
        ],
        out_specs=pl.BlockSpec((rows, d), lambda e, h, f: (e * n_half + h, 0), pipeline_mode=pl.Buffered(1)),
        out_shape=jax.ShapeDtypeStruct((n_exp * batch * cap, d), BF16),
        scratch_shapes=[
            pltpu.VMEM((rows, width), F32),
            pltpu.VMEM((rows, d), BF16),
            pltpu.VMEM((rows, 1), F32),
            pltpu.VMEM((rows, d), F32),
            pltpu.VMEM((d, tf), BF16),
            pltpu.VMEM((d, tf), BF16),
            pltpu.VMEM((tf, d), BF16),
            pltpu.SemaphoreType.DMA((1,)),
        ],
        compiler_params=pltpu.CompilerParams(dimension_semantics=("arbitrary",) * 3,
                                             vmem_limit_bytes=MOE_EXPERT_VMEM_LIMIT_BYTES),
        name="moe_experts",
    )(idx, idx, x1e, w1, w3, w2)


def _moe_combine_kernel(alpha, n_exp, batch, cap, tab_ref, y_hbm, tok_hbm, x_ref, g_ref, b_ref, o_ref,
                        ybuf, tokbuf, acc, nwin_ref, sem):
    b, i = pl.program_id(0), pl.program_id(1)
    n_tiles = pl.num_programs(1)
    tm = x_ref.shape[0]
    step = b * n_tiles + i
    slot = step % 2

    def window_copies(src_y, src_t, buf, off):
        dst = pl.ds(pl.multiple_of(off * MOE_WIN, MOE_WIN), MOE_WIN)
        return (pltpu.make_async_copy(y_hbm.at[pl.ds(pl.multiple_of(src_y, MOE_WIN), MOE_WIN)],
                                      ybuf.at[buf, dst], sem.at[buf, 0]),
                pltpu.make_async_copy(tok_hbm.at[pl.ds(pl.multiple_of(src_t, MOE_WIN), MOE_WIN)],
                                      tokbuf.at[buf, dst], sem.at[buf, 1]))

    def fetch_tile(bb, ii, buf):
        def per_expert(e, off):
            base = (bb * n_exp + e) * MOE_TAB + ii
            s0, s1 = tab_ref[base], tab_ref[base + 1]
            w0 = s0 // MOE_WIN
            nw = jnp.where(s1 > s0, (s1 + MOE_WIN - 1) // MOE_WIN - w0, 0)

            def per_window(w, off2):
                for cp in window_copies((e * batch + bb) * cap + (w0 + w) * MOE_WIN,
                                        (bb * n_exp + e) * cap + (w0 + w) * MOE_WIN, buf, off2):
                    cp.start()
                return off2 + 1

            return lax.fori_loop(0, nw, per_window, off)

        nwin_ref[buf] = lax.fori_loop(0, n_exp, per_expert, 0)

    @pl.when(step == 0)
    def _():
        ybuf[...] = jnp.zeros_like(ybuf)
        tokbuf[...] = jnp.zeros_like(tokbuf)
        fetch_tile(b, i, slot)

    nwin = nwin_ref[slot]

    def wait_window(w, c):
        for cp in window_copies(0, 0, slot, w):
            cp.wait()
        return c

    lax.fori_loop(0, nwin, wait_window, 0)

    @pl.when(step + 1 < pl.num_programs(0) * n_tiles)
    def _():
        nxt = step + 1
        fetch_tile(nxt // n_tiles, nxt % n_tiles, 1 - slot)

    nrows = nwin * MOE_WIN
    acc[...] = jnp.zeros_like(acc)
    sub = lax.broadcasted_iota(I32, (MOE_TILE, LANES), 0)
    lane = lax.broadcasted_iota(I32, (MOE_TILE, LANES), 1)

    def chunk(c, carry):
        rows = pl.ds(pl.multiple_of(c * MOE_TILE, MOE_TILE), MOE_TILE)
        t_local = tokbuf[slot, rows, :] - i * tm
        valid = (sub + c * MOE_TILE) < nrows
        halves = [jnp.where(valid & (t_local == lane + k * LANES), 1.0, 0.0) for k in range(tm // LANES)]
        onehot = jnp.transpose(jnp.concatenate(halves, axis=1)).astype(BF16)
        for cc in range(0, acc.shape[1], DOT_TN):
            acc[:, cc:cc + DOT_TN] += jnp.dot(onehot, ybuf[slot, rows, cc:cc + DOT_TN], preferred_element_type=F32)
        return carry

    lax.fori_loop(0, (nrows + MOE_TILE - 1) // MOE_TILE, chunk, 0)
    o_ref[...] = _layer_norm_rows(alpha * x_ref[...] + acc[...], g_ref[...], b_ref[...])


def _moe_combine(x1e, y, tok, tab, g, b, alpha, batch, seq, cap, n_exp):
    t = x1e.shape[0]
    d = y.shape[1]
    tm = MOE_TILE
    n_tiles = seq // tm
    max_rows = n_exp * (tm + MOE_WIN)
    max_rows = (max_rows + MOE_TILE - 1) // MOE_TILE * MOE_TILE
    return pl.pallas_call(
        functools.partial(_moe_combine_kernel, alpha, n_exp, batch, cap),
        grid_spec=pltpu.PrefetchScalarGridSpec(
            num_scalar_prefetch=1,
            grid=(batch, n_tiles),
            in_specs=[
                pl.BlockSpec(memory_space=pl.ANY),
                pl.BlockSpec(memory_space=pl.ANY),
                pl.BlockSpec((tm, d), lambda bb, i, tab_r: (bb * n_tiles + i, 0)),
                pl.BlockSpec((1, d), lambda bb, i, tab_r: (0, 0)),
                pl.BlockSpec((1, d), lambda bb, i, tab_r: (0, 0)),
            ],
            out_specs=pl.BlockSpec((tm, d), lambda bb, i, tab_r: (bb * n_tiles + i, 0)),
            scratch_shapes=[
                pltpu.VMEM((2, max_rows, d), BF16),
                pltpu.VMEM((2, max_rows, LANES), I32),
                pltpu.VMEM((tm, d), F32),
                pltpu.SMEM((2,), I32),
                pltpu.SemaphoreType.DMA((2, 2)),
            ],
        ),
        out_shape=jax.ShapeDtypeStruct((t, d), F32),
        compiler_params=_cparams("arbitrary", "arbitrary"),
        name="moe_combine",
    )(tab.reshape(-1), y, tok, x1e, g, b)


def _moe_block(x1e, afft, g, b, w1, w3, w2, layer, alpha, batch, seq):
    n_exp = w1.shape[1]
    cap = EC_CAPACITY_FACTOR * seq // n_exp
    idx, tok, tab = _moe_select(afft, batch, seq, cap)
    y = _moe_experts(x1e, idx, w1, w3, w2, layer, batch, seq, cap, tf=MOE_EXPERT_FF_TILE)
    return _moe_combine(x1e, y, tok, tab, g, b, alpha, batch, seq, cap, n_exp)


def kernel(x, mix_w_in, mix_w_out, hy_conv_w, hy_conv_b, hy_ffn_w1, hy_ffn_b1, hy_ffn_w2, hy_ffn_b2, hy_ffn_w3,
           hy_ffn_b3, hy_ffn_w4, hy_sin_freq, hy_bias, pool_w, pool_scale, ln_mix_g, ln_mix_b, ln_ffn_g, ln_ffn_b,
           router_w, exp_w1, exp_w3, exp_w2):
    batch, seq, d = x.shape
    depth = ln_mix_g.shape[0]
    alpha = (2 * depth) ** 0.25
    n_exp = router_w.shape[2]
    xt = x.reshape(batch * seq, d)
    row = lambda v: v[None, :]
    for layer in range(depth):
        i = layer // 2
        rw_pad = jnp.pad(router_w[layer], ((0, 0), (0, LANES - n_exp)))
        if layer % 2 == 0:
            proj = _matmul([xt], mix_w_in[i], BF16, 1024, 1024)
            attn = _dilated_attention(proj, batch, seq)
            hcat = _hyena_filters(seq, hy_ffn_w1[i], hy_ffn_b1[i], hy_ffn_w2[i], hy_ffn_b2[i], hy_ffn_w3[i],
                                  hy_ffn_b3[i], hy_ffn_w4[i], hy_sin_freq[i])
            z, x0c = _hyena_pre(proj, hy_conv_w[i], hy_conv_b[i], seq)
            cosm, nsinm = _dft_matrices(seq)
            k_re, k_im = _dft_filter(cosm, nsinm, hcat, 2048)
            p_re, p_im = _dft_fwd(cosm, nsinm, z, k_re, k_im, batch, 2048)
            hyena = _dft_inv(cosm, nsinm, p_re, p_im, z, x0c, hy_bias[i], batch, 1024, 2048)
            x1e, afft = _outproj_ln_router(attn, hyena, mix_w_out[i], xt, row(ln_mix_g[layer]), row(ln_mix_b[layer]),
                                           rw_pad, alpha, n_exp)
        else:
            x1e, afft = _pool_ln_router(xt, pool_w[i], row(pool_scale[i]), row(ln_mix_g[layer]), row(ln_mix_b[layer]),
                                        rw_pad, alpha, n_exp, seq)
        xt = _moe_block(x1e, afft, row(ln_ffn_g[layer]), row(ln_ffn_b[layer]), exp_w1, exp_w3, exp_w2, layer,
                        alpha, batch, seq)
    return xt.reshape(batch, seq, d)
```

```python
import functools
import math

import numpy as np
import jax
import jax.numpy as jnp
from jax import lax
from jax.experimental import pallas as pl
from jax.experimental.pallas import tpu as pltpu

F32 = jnp.float32
BF16 = jnp.bfloat16
I32 = jnp.int32

LANES = 128
SUBLANES = 8
VMEM_LIMIT_BYTES = 56 * 1024 * 1024

ATTN_HEADS = 8
HEAD_DIM = 128
DILATED_BRANCHES = ((128, 1), (512, 4), (2048, 16))
POOL_WINDOWS = (2, 4, 8, 16)
N_EXPERTS = 16
EC_CAPACITY_FACTOR = 2
HYENA_POS_DIM = 33
HYENA_DECAY_FAST = 0.3
HYENA_DECAY_SLOW = 1.5
HYENA_DECAY_TARGET = 1e-2
LN_EPS = 1e-5
NEG_BIG = -1e30


def _cparams(*sem):
    return pltpu.CompilerParams(dimension_semantics=sem, vmem_limit_bytes=VMEM_LIMIT_BYTES)


DOT_TM = 256
DOT_TN = 512


def _dot_tiles(a_ref, b_ref, emit, cast=False):
    m, n = a_ref.shape[0], b_ref.shape[1]
    tm, tn = min(DOT_TM, m), min(DOT_TN, n)
    for r in range(0, m, tm):
        a = a_ref[r:r + tm, :]
        a = a.astype(BF16) if cast else a
        for c in range(0, n, tn):
            emit(slice(r, r + tm), slice(c, c + tn), jnp.dot(a, b_ref[:, c:c + tn], preferred_element_type=F32))


def _mm_kernel(n_in, *refs):
    x_refs, w_ref, o_ref, wb_ref = refs[:n_in], refs[n_in], refs[n_in + 1], refs[n_in + 2]

    @pl.when(pl.program_id(1) == 0)
    def _():
        for r in range(0, w_ref.shape[0], DOT_TM):
            wb_ref[r:r + DOT_TM, :] = w_ref[r:r + DOT_TM, :].astype(BF16)

    tm, tn = x_refs[0].shape[0], o_ref.shape[1]
    for r in range(0, tm, DOT_TM):
        xs = [x_ref[r:r + DOT_TM, :].astype(BF16) for x_ref in x_refs]
        for c in range(0, tn, DOT_TN):
            k0, v = 0, None
            for xv in xs:
                part = jnp.dot(xv, wb_ref[k0:k0 + xv.shape[1], c:c + DOT_TN], preferred_element_type=F32)
                v = part if v is None else v + part
                k0 += xv.shape[1]
            o_ref[r:r + DOT_TM, c:c + DOT_TN] = v.astype(o_ref.dtype)


def _matmul(xs, w, out_dtype, tm, tn):
    m = xs[0].shape[0]
    k, n = w.shape
    assert sum(x.shape[1] for x in xs) == k
    return pl.pallas_call(
        functools.partial(_mm_kernel, len(xs)),
        grid=(n // tn, m // tm),
        in_specs=[pl.BlockSpec((tm, x.shape[1]), lambda j, i: (i, 0)) for x in xs]
        + [pl.BlockSpec((k, tn), lambda j, i: (0, j))],
        out_specs=pl.BlockSpec((tm, tn), lambda j, i: (i, j)),
        out_shape=jax.ShapeDtypeStruct((m, n), out_dtype),
        scratch_shapes=[pltpu.VMEM((k, tn), BF16)],
        compiler_params=_cparams("arbitrary", "arbitrary"),
        name="matmul",
    )(*xs, w)


def _layer_norm_rows(v, g, b):
    mu = jnp.mean(v, axis=-1, keepdims=True)
    c = v - mu
    var = jnp.mean(c * c, axis=-1, keepdims=True)
    return c * lax.rsqrt(var + LN_EPS) * g + b


def _bf16_pieces(v):
    hi = v.astype(BF16)
    return hi, (v - hi.astype(F32)).astype(BF16)


def _router_affinity(x1, rw_ref, n_experts):
    xh, xl = _bf16_pieces(x1)
    wh, wl = _bf16_pieces(rw_ref[...])
    dot = functools.partial(jnp.dot, preferred_element_type=F32)
    logits = dot(xh, wh) + (dot(xh, wl) + dot(xl, wh))
    lane = lax.broadcasted_iota(I32, logits.shape, 1)
    logits = jnp.where(lane < n_experts, logits, NEG_BIG)
    mx = jnp.max(logits, axis=-1, keepdims=True)
    e = jnp.exp(logits - mx)
    return e / jnp.sum(e, axis=-1, keepdims=True)


def _ln_router_rows(alpha, n_experts, rows, x_ref, mix_ref, g_ref, b_ref, rw_ref, xe_ref, afft_ref):
    d = x_ref.shape[1]
    x1 = _layer_norm_rows(alpha * x_ref[rows, :] + mix_ref[rows, :], g_ref[...], b_ref[...])
    aff = _router_affinity(x1, rw_ref, n_experts)
    xe_ref[rows, :d] = x1
    xe_ref[rows, d:] = aff
    afft_ref[:, rows] = jnp.transpose(aff)[:n_experts, :]


def _outproj_ln_router_kernel(alpha, n_experts, a_ref, h_ref, w_ref, x_ref, g_ref, b_ref, rw_ref, xe_ref, afft_ref,
                              wb_ref, mix_ref):
    @pl.when(pl.program_id(0) == 0)
    def _():
        for r in range(0, w_ref.shape[0], DOT_TM):
            wb_ref[r:r + DOT_TM, :] = w_ref[r:r + DOT_TM, :].astype(BF16)

    tm, ka = a_ref.shape
    d = x_ref.shape[1]
    dot = functools.partial(jnp.dot, preferred_element_type=F32)
    for r in range(0, tm, DOT_TM):
        rows = slice(r, r + DOT_TM)
        av, hv = a_ref[rows, :], h_ref[rows, :]
        for c in range(0, d, DOT_TN):
            cols = slice(c, c + DOT_TN)
            mix_ref[rows, cols] = dot(av, wb_ref[:ka, cols]) + dot(hv, wb_ref[ka:, cols])
        _ln_router_rows(alpha, n_experts, rows, x_ref, mix_ref, g_ref, b_ref, rw_ref, xe_ref, afft_ref)


def _outproj_ln_router(attn, hyena, w_out, x, g, b, rw_pad, alpha, n_experts, tm=512):
    t, d = x.shape
    const = lambda a: pl.BlockSpec(a.shape, lambda i: (0,) * a.ndim)
    return pl.pallas_call(
        functools.partial(_outproj_ln_router_kernel, alpha, n_experts),
        grid=(t // tm,),
        in_specs=[
            pl.BlockSpec((tm, attn.shape[1]), lambda i: (i, 0)),
            pl.BlockSpec((tm, hyena.shape[1]), lambda i: (i, 0)),
            pl.BlockSpec(w_out.shape, lambda i: (0, 0), pipeline_mode=pl.Buffered(1)),
            pl.BlockSpec((tm, d), lambda i: (i, 0)),
            const(g), const(b), const(rw_pad),
        ],
        out_specs=[pl.BlockSpec((tm, d + LANES), lambda i: (i, 0)), pl.BlockSpec((n_experts, tm), lambda i: (0, i))],
        out_shape=[jax.ShapeDtypeStruct((t, d + LANES), F32), jax.ShapeDtypeStruct((n_experts, t), F32)],
        scratch_shapes=[pltpu.VMEM(w_out.shape, BF16), pltpu.VMEM((tm, d), F32)],
        compiler_params=_cparams("arbitrary"),
        name="outproj_ln_router",
    )(attn, hyena, w_out, x, g, b, rw_pad)


ATTN_QBLK = 128
ATTN_UNROLL = 32
ATTN_HALF = 64
ATTN_WINDOW_LEADS = (0, ATTN_HALF, 2 * ATTN_HALF)


def _attn_branch(slope, dil, lc, q_cm, k_cm, v_cm, o_cm, l_cm, seq, bias_ref):
    win = min(ATTN_QBLK + 2 * ATTN_HALF, lc)
    blocks_per_class = lc // ATTN_QBLK
    scale = 1.0 / math.sqrt(HEAD_DIM)
    diff = lax.broadcasted_iota(I32, (ATTN_QBLK, win), 1) - lax.broadcasted_iota(I32, (ATTN_QBLK, win), 0)
    for case in range(len(ATTN_WINDOW_LEADS)):
        rel = jnp.abs(diff - ATTN_WINDOW_LEADS[case])
        bias_ref[case, :, :win] = jnp.where(rel <= ATTN_HALF, -(slope * dil) * rel.astype(F32), NEG_BIG)

    def body(blk, carry):
        cls = blk // blocks_per_class
        q0 = (blk % blocks_per_class) * ATTN_QBLK
        w0 = jnp.clip(q0 - ATTN_HALF, 0, lc - win)
        qs = pl.multiple_of(cls * lc + q0, ATTN_HALF)
        ks = pl.multiple_of(cls * lc + w0, ATTN_HALF)
        q = q_cm[pl.ds(qs, ATTN_QBLK), :]
        k = k_cm[pl.ds(ks, win), :]
        v = v_cm[pl.ds(ks, win), :]
        s = lax.dot_general(q, k, (((1,), (1,)), ((), ())), preferred_element_type=F32) * scale
        bias = bias_ref[(q0 - w0) // ATTN_HALF, :, :win]
        s = jnp.where(bias > 0.5 * NEG_BIG, s + bias, NEG_BIG)
        m = jnp.max(s, axis=-1, keepdims=True)
        p = jnp.exp(s - m)
        l = jnp.sum(p, axis=-1, keepdims=True)
        o = jnp.dot(p.astype(BF16), v, preferred_element_type=F32) / l
        o_cm[pl.ds(qs, ATTN_QBLK), :] = o
        l_cm[pl.ds(qs, ATTN_QBLK), :] = jnp.broadcast_to(m + jnp.log(l), (ATTN_QBLK, HEAD_DIM))
        return carry

    lax.fori_loop(0, seq // ATTN_QBLK, body, 0, unroll=ATTN_UNROLL)


def _attn_kernel(slopes_ref, q_ref, k_ref, v_ref, out_ref, qf, kf, vf, qc, kc, vc, ocm, lcm, onat, lnat, bias_ref):
    seq = q_ref.shape[0]
    slope = slopes_ref[pl.program_id(1)]
    qf[...] = q_ref[...].astype(F32)
    kf[...] = k_ref[...].astype(F32)
    vf[...] = v_ref[...].astype(F32)
    for bi, (window, dil) in enumerate(DILATED_BRANCHES):
        assert window // (2 * dil) == ATTN_HALF
        lc = seq // dil
        if dil == 1:
            _attn_branch(slope, dil, lc, q_ref, k_ref, v_ref, onat.at[bi], lnat.at[bi], seq, bias_ref)
            continue
        for r in range(dil):
            qc[r * lc:(r + 1) * lc, :] = qf[pl.ds(r, lc, stride=dil), :].astype(BF16)
            kc[r * lc:(r + 1) * lc, :] = kf[pl.ds(r, lc, stride=dil), :].astype(BF16)
            vc[r * lc:(r + 1) * lc, :] = vf[pl.ds(r, lc, stride=dil), :].astype(BF16)
        _attn_branch(slope, dil, lc, qc, kc, vc, ocm, lcm, seq, bias_ref)
        for r in range(dil):
            onat[bi, pl.ds(r, lc, stride=dil), :] = ocm[r * lc:(r + 1) * lc, :]
            lnat[bi, pl.ds(r, lc, stride=dil), :] = lcm[r * lc:(r + 1) * lc, :]
    lses = [lnat[bi] for bi in range(len(DILATED_BRANCHES))]
    mx = functools.reduce(jnp.maximum, lses)
    ws = [jnp.exp(l - mx) for l in lses]
    num = sum(w * onat[bi] for bi, w in enumerate(ws))
    out_ref[...] = (num / sum(ws)).astype(out_ref.dtype)


def _dilated_attention(proj, batch, seq):
    t = proj.shape[0]
    slopes = jnp.asarray(2.0 ** (-(8.0 / ATTN_HEADS) * np.arange(1, ATTN_HEADS + 1)), F32)
    blk = (seq, HEAD_DIM)
    nb = len(DILATED_BRANCHES)
    return pl.pallas_call(
        _attn_kernel,
        grid_spec=pltpu.PrefetchScalarGridSpec(
            num_scalar_prefetch=1,
            grid=(batch, ATTN_HEADS),
            in_specs=[
                pl.BlockSpec(blk, lambda b, h, s: (b, h)),
                pl.BlockSpec(blk, lambda b, h, s: (b, ATTN_HEADS + h)),
                pl.BlockSpec(blk, lambda b, h, s: (b, 2 * ATTN_HEADS + h)),
            ],
            out_specs=pl.BlockSpec(blk, lambda b, h, s: (b, h)),
            scratch_shapes=[pltpu.VMEM(blk, F32)] * 3 + [pltpu.VMEM(blk, BF16)] * 3 + [pltpu.VMEM(blk, F32)] * 2
            + [pltpu.VMEM((nb,) + blk, F32)] * 2
            + [pltpu.VMEM((len(ATTN_WINDOW_LEADS), ATTN_QBLK, ATTN_QBLK + 2 * ATTN_HALF), F32)],
        ),
        out_shape=jax.ShapeDtypeStruct((t, ATTN_HEADS * HEAD_DIM), BF16),
        compiler_params=_cparams("arbitrary", "arbitrary"),
        name="dilated_attention",
    )(slopes, proj, proj, proj)


HYENA_ORDER_PAD = LANES


def _filter_kernel(z_ref, w1, b1, w2, b2, w3, b3, w4, fr, absdelta_ref, h_ref):
    hi = lax.Precision.HIGHEST
    f = fr[...]
    z = z_ref[...]
    h = jnp.sin(f * (jnp.dot(z, w1[...], preferred_element_type=F32, precision=hi) + b1[...]))
    h = jnp.sin(f * (jnp.dot(h, w2[...], preferred_element_type=F32, precision=hi) + b2[...]))
    h = jnp.sin(f * (jnp.dot(h, w3[...], preferred_element_type=F32, precision=hi) + b3[...]))
    h = jnp.dot(h, w4[...], preferred_element_type=F32, precision=hi)
    t = z[:, 0:1]
    h = h * jnp.exp(-t * absdelta_ref[...])
    c = h.shape[1] // 2
    row = lax.broadcasted_iota(I32, h.shape, 0) + pl.program_id(0) * h.shape[0]
    col = lax.broadcasted_iota(I32, h.shape, 1)
    h_ref[...] = jnp.where((row == 0) & (col >= c), 0.0, h).astype(h_ref.dtype)


def _position_features(seq):
    t = np.linspace(0.0, 1.0, seq)[:, None]
    bands = (HYENA_POS_DIM - 1) // 2
    w_ang = 2.0 * np.pi * np.arange(seq)[:, None] / seq
    f = np.linspace(1e-4, bands - 1, bands)[None, :]
    z = np.concatenate([t, np.cos(f * w_ang), -np.sin(f * w_ang)], axis=-1)
    return np.pad(z, ((0, 0), (0, HYENA_ORDER_PAD - z.shape[1]))).astype(np.float32)


def _pad2(a, rows, cols):
    return jnp.pad(a, ((0, rows - a.shape[0]), (0, cols - a.shape[1])))


def _hyena_filters(seq, w1, b1, w2, b2, w3, b3, w4, freq, tl=512):
    p = HYENA_ORDER_PAD
    c2 = w4.shape[1]
    c = c2 // 2
    max_decay = math.log(HYENA_DECAY_TARGET) / HYENA_DECAY_FAST
    min_decay = math.log(HYENA_DECAY_TARGET) / HYENA_DECAY_SLOW
    absdelta = np.abs(np.linspace(min_decay, max_decay, c))
    absdelta = jnp.asarray(np.concatenate([absdelta, absdelta])[None, :], F32)
    args = (
        jnp.asarray(_position_features(seq)),
        _pad2(w1, p, p), _pad2(b1[None, :], 1, p), _pad2(w2, p, p), _pad2(b2[None, :], 1, p),
        _pad2(w3, p, p), _pad2(b3[None, :], 1, p), _pad2(w4, p, c2), _pad2(freq[None, :], 1, p), absdelta,
    )
    full = lambda a: pl.BlockSpec(a.shape, lambda i: (0, 0))
    return pl.pallas_call(
        _filter_kernel,
        grid=(seq // tl,),
        in_specs=[pl.BlockSpec((tl, p), lambda i: (i, 0))] + [full(a) for a in args[1:]],
        out_specs=pl.BlockSpec((tl, c2), lambda i: (i, 0)),
        out_shape=jax.ShapeDtypeStruct((seq, c2), BF16),
        compiler_params=_cparams("arbitrary"),
        name="hyena_filters",
    )(*args)


def _shift_rows(u, edge_row, down):
    n = u.shape[0]
    row = lax.broadcasted_iota(I32, u.shape, 0)
    if down:
        return jnp.where(row == 0, edge_row, pltpu.roll(u, 1, 0))
    return jnp.where(row == n - 1, edge_row, pltpu.roll(u, n - 1, 0))


def _hyena_pre_kernel(tiles_per_seq, cur_ref, prev_ref, next_ref, w_ref, b_ref, z_ref, x0_ref):
    i = pl.program_id(0) % tiles_per_seq
    u = cur_ref[...].astype(F32)
    prev_row = jnp.where(i == 0, 0.0, prev_ref[SUBLANES - 1:SUBLANES, :].astype(F32))
    next_row = jnp.where(i == tiles_per_seq - 1, 0.0, next_ref[0:1, :].astype(F32))
    y = (w_ref[0:1, :] * _shift_rows(u, prev_row, True) + w_ref[1:2, :] * u
         + w_ref[2:3, :] * _shift_rows(u, next_row, False) + b_ref[...])
    c = y.shape[1] // 3
    x0_ref[...] = y[:, :c].astype(x0_ref.dtype)
    z_ref[...] = (y[:, 2 * c:] * y[:, c:2 * c]).astype(z_ref.dtype)


def _hyena_pre(proj, conv_w, conv_b, seq, ts=256):
    t, width = proj.shape
    c3 = conv_w.shape[1]
    assert width == 2 * c3
    c = c3 // 3
    tiles_per_seq = seq // ts
    r8 = ts // SUBLANES
    last8 = t // SUBLANES - 1
    return pl.pallas_call(
        functools.partial(_hyena_pre_kernel, tiles_per_seq),
        grid=(t // ts,),
        in_specs=[
            pl.BlockSpec((ts, c3), lambda i: (i, 1)),
            pl.BlockSpec((SUBLANES, c3), lambda i: (jnp.maximum(i * r8 - 1, 0), 1)),
            pl.BlockSpec((SUBLANES, c3), lambda i: (jnp.minimum((i + 1) * r8, last8), 1)),
            pl.BlockSpec((3, c3), lambda i: (0, 0)),
            pl.BlockSpec((1, c3), lambda i: (0, 0)),
        ],
        out_specs=[pl.BlockSpec((ts, c), lambda i: (i, 0))] * 2,
        out_shape=[jax.ShapeDtypeStruct((t, c), BF16)] * 2,
        compiler_params=_cparams("arbitrary"),
        name="hyena_short_conv",
    )(proj, proj, proj, conv_w, conv_b[None, :])


DFT_FBLK = 512
DFT_SPLIT = 64
DFT_GEN_ROWS = 256
DFT_EPI_ROWS = 64
BF16_ROWS = 2 * SUBLANES


def _dft_tables(seq):
    n = 2 * seq
    k = np.arange(seq)[:, None]
    a = np.arange(seq // DFT_SPLIT)[None, :]
    b = np.arange(DFT_SPLIT)[None, :]
    ang1 = 2.0 * np.pi * ((k * a * DFT_SPLIT) % n) / n
    ang2 = 2.0 * np.pi * ((k * b) % n) / n
    return tuple(jnp.asarray(f(a_), F32) for a_ in (ang1, ang2) for f in (np.cos, np.sin))


def _dft_gen_kernel(c1_ref, s1_ref, c2_ref, s2_ref, cos_ref, nsin_ref):
    rows, na = c1_ref.shape
    seq = cos_ref.shape[1]
    reps = DOT_TN // DFT_SPLIT
    c2 = jnp.concatenate([c2_ref[...]] * reps, axis=1)
    s2 = jnp.concatenate([s2_ref[...]] * reps, axis=1)

    c1p, s1p = _bf16_pieces(c1_ref[...]), _bf16_pieces(s1_ref[...])
    for c in range(0, seq, DOT_TN):
        n_of_lane = lax.broadcasted_iota(I32, (na, DOT_TN), 1) + c
        expand = jnp.where(n_of_lane // DFT_SPLIT == lax.broadcasted_iota(I32, (na, DOT_TN), 0), 1.0, 0.0).astype(BF16)
        a, b = (jnp.dot(hi, expand, preferred_element_type=F32) + jnp.dot(lo, expand, preferred_element_type=F32)
                for hi, lo in (c1p, s1p))
        cos_ref[:, c:c + DOT_TN] = (a * c2 - b * s2).astype(cos_ref.dtype)
        nsin_ref[:, c:c + DOT_TN] = (-(b * c2 + a * s2)).astype(nsin_ref.dtype)


def _dft_matrices(seq):
    tables = _dft_tables(seq)
    tr = DFT_GEN_ROWS
    return pl.pallas_call(
        _dft_gen_kernel,
        grid=(seq // tr,),
        in_specs=[pl.BlockSpec((tr, t.shape[1]), lambda i: (i, 0)) for t in tables],
        out_specs=[pl.BlockSpec((tr, seq), lambda i: (i, 0))] * 2,
        out_shape=[jax.ShapeDtypeStruct((seq, seq), BF16)] * 2,
        compiler_params=_cparams("arbitrary"),
        name="dft_matrices",
    )(*tables)


def _nyquist_rows(tk):
    lane = lax.broadcasted_iota(I32, (SUBLANES, tk), 1)
    return jnp.where(lane % 2 == 0, 1.0, -1.0).astype(BF16)


def _dft_accumulate(cos_ref, nsin_ref, u_ref, re_ref, im_ref, ny_ref, kk, first_block):
    @pl.when(kk == 0)
    def _():
        re_ref[...] = jnp.zeros_like(re_ref)
        im_ref[...] = jnp.zeros_like(im_ref)
        ny_ref[...] = jnp.zeros_like(ny_ref)

    def add_re(rows, cols, v):
        re_ref[rows, cols] += v

    def add_im(rows, cols, v):
        im_ref[rows, cols] += v

    _dot_tiles(cos_ref, u_ref, add_re)
    _dot_tiles(nsin_ref, u_ref, add_im)

    @pl.when(first_block)
    def _():
        ny_ref[...] += jnp.dot(_nyquist_rows(u_ref.shape[0]), u_ref[...], preferred_element_type=F32)


def _dft_filter_kernel(cos_ref, nsin_ref, h_ref, kre_ref, kim_ref, re_ref, im_ref, ny_ref):
    i, kk = pl.program_id(0), pl.program_id(1)
    _dft_accumulate(cos_ref, nsin_ref, h_ref, re_ref, im_ref, ny_ref, kk, i == 0)

    @pl.when(kk == pl.num_programs(1) - 1)
    def _():
        c = kre_ref.shape[1]
        kre_ref[...] = re_ref[:, :c] + re_ref[:, c:]
        kim_ref[...] = im_ref[:, :c] - im_ref[:, c:]

        @pl.when(i == 0)
        def _():
            kim_ref[0:1, :] = ny_ref[0:1, :c] + ny_ref[0:1, c:]


def _dft_filter(cosm, nsinm, hcat, tk):
    seq = cosm.shape[0]
    c2 = hcat.shape[1]
    mat = pl.BlockSpec((DFT_FBLK, tk), lambda i, kk: (i, kk))
    return pl.pallas_call(
        _dft_filter_kernel,
        grid=(seq // DFT_FBLK, seq // tk),
        in_specs=[mat, mat, pl.BlockSpec((tk, c2), lambda i, kk: (kk, 0))],
        out_specs=[pl.BlockSpec((DFT_FBLK, c2 // 2), lambda i, kk: (i, 0))] * 2,
        out_shape=[jax.ShapeDtypeStruct((seq, c2 // 2), F32)] * 2,
        scratch_shapes=[pltpu.VMEM((DFT_FBLK, c2), F32)] * 2 + [pltpu.VMEM((SUBLANES, c2), F32)],
        compiler_params=_cparams("arbitrary", "arbitrary"),
        name="dft_filter",
    )(cosm, nsinm, hcat)


def _dft_fwd_kernel(seq, cos_ref, nsin_ref, z_ref, kre_ref, kim_ref, pre_ref, pim_ref, re_ref, im_ref, ny_ref):
    i, kk = pl.program_id(1), pl.program_id(2)
    _dft_accumulate(cos_ref, nsin_ref, z_ref, re_ref, im_ref, ny_ref, kk, i == 0)

    @pl.when(kk == pl.num_programs(2) - 1)
    def _():
        inv_n = 1.0 / (2 * seq)
        for r in range(0, DFT_FBLK, DFT_EPI_ROWS):
            rs = slice(r, r + DFT_EPI_ROWS)
            ure, uim, kre, kim = re_ref[rs, :], im_ref[rs, :], kre_ref[rs, :], kim_ref[rs, :]
            pre_ref[rs, :] = ((ure * kre - uim * kim) * (2 * inv_n)).astype(pre_ref.dtype)
            pim_ref[rs, :] = ((ure * kim + uim * kre) * (2 * inv_n)).astype(pim_ref.dtype)

        @pl.when(i == 0)
        def _():
            head = slice(0, BF16_ROWS)
            ure, uim, kre, kim = re_ref[head, :], im_ref[head, :], kre_ref[head, :], kim_ref[head, :]
            first = lax.broadcasted_iota(I32, ure.shape, 0) == 0
            p_re = jnp.where(first, ure * kre * inv_n, (ure * kre - uim * kim) * (2 * inv_n))
            p_im = jnp.where(first, ny_ref[0:1, :] * kim * inv_n, (ure * kim + uim * kre) * (2 * inv_n))
            pre_ref[head, :] = p_re.astype(pre_ref.dtype)
            pim_ref[head, :] = p_im.astype(pim_ref.dtype)


def _dft_fwd(cosm, nsinm, z, kre, kim, batch, tk):
    seq = cosm.shape[0]
    c = z.shape[1]
    nfb = seq // DFT_FBLK
    nk = seq // tk
    mat = pl.BlockSpec((DFT_FBLK, tk), lambda b, i, kk: (i, kk))
    spec = pl.BlockSpec((DFT_FBLK, c), lambda b, i, kk: (i, 0))
    return pl.pallas_call(
        functools.partial(_dft_fwd_kernel, seq),
        grid=(batch, nfb, nk),
        in_specs=[mat, mat, pl.BlockSpec((tk, c), lambda b, i, kk: (b * nk + kk, 0)), spec, spec],
        out_specs=[pl.BlockSpec((DFT_FBLK, c), lambda b, i, kk: (b * nfb + i, 0))] * 2,
        out_shape=[jax.ShapeDtypeStruct((batch * seq, c), BF16)] * 2,
        scratch_shapes=[pltpu.VMEM((DFT_FBLK, c), F32)] * 2 + [pltpu.VMEM((SUBLANES, c), F32)],
        compiler_params=_cparams("arbitrary", "arbitrary", "arbitrary"),
        name="dft_fwd",
    )(cosm, nsinm, z, kre, kim)


def _dft_inv_kernel(cos_ref, nsin_ref, pre_ref, pim_ref, pny_ref, z_ref, x0_ref, bias_ref, o_ref, acc_ref):
    it, kk = pl.program_id(1), pl.program_id(2)
    tt = acc_ref.shape[0]

    @pl.when(kk == 0)
    def _():
        acc_ref[...] = jnp.zeros_like(acc_ref)

    def accumulate(rows, cols, v):
        acc_ref[rows, cols] += v

    _dot_tiles(cos_ref, pre_ref, accumulate)
    _dot_tiles(nsin_ref, pim_ref, accumulate)

    @pl.when(kk == pl.num_programs(2) - 1)
    def _():
        nyq = pny_ref[0:1, :].astype(F32)
        for r in range(0, tt, DOT_TM):
            rs = slice(r, r + DOT_TM)
            t = it * tt + r + lax.broadcasted_iota(I32, (DOT_TM, 1), 0)
            y = acc_ref[rs, :] + jnp.where(t % 2 == 0, 1.0, -1.0) * nyq
            zz = z_ref[rs, :].astype(F32)
            o_ref[rs, :] = (x0_ref[rs, :].astype(F32) * (y + zz * bias_ref[...])).astype(o_ref.dtype)


def _dft_inv(cosm, nsinm, p_re, p_im, z, x0c, hy_bias, batch, tt, tk):
    seq = cosm.shape[0]
    c = z.shape[1]
    nt = seq // tt
    nk = seq // tk
    mat = pl.BlockSpec((tt, tk), lambda b, i, kk: (i, kk))
    spec = pl.BlockSpec((tk, c), lambda b, i, kk: (b * nk + kk, 0))
    tile = pl.BlockSpec((tt, c), lambda b, i, kk: (b * nt + i, 0))
    return pl.pallas_call(
        _dft_inv_kernel,
        grid=(batch, nt, nk),
        in_specs=[mat, mat, spec, spec, pl.BlockSpec((BF16_ROWS, c), lambda b, i, kk: (b * (seq // BF16_ROWS), 0)),
                  tile, tile, pl.BlockSpec((1, c), lambda b, i, kk: (0, 0))],
        out_specs=tile,
        out_shape=jax.ShapeDtypeStruct((batch * seq, c), BF16),
        scratch_shapes=[pltpu.VMEM((tt, c), F32)],
        compiler_params=_cparams("arbitrary", "arbitrary", "arbitrary"),
        name="dft_inv",
    )(cosm, nsinm, p_re, p_im, p_im, z, x0c, hy_bias[None, :])


POOL_HALO = SUBLANES


def _pool_ln_router_kernel(alpha, n_experts, seq, x_ref, prev_ref, next_ref, pw_ref, ps_ref, g_ref, b_ref, rw_ref,
                           xe_ref, afft_ref, ext_ref, mix_ref):
    tm, d = x_ref.shape
    tiles_per_seq = seq // tm
    i = pl.program_id(0) % tiles_per_seq
    ext_ref[:POOL_HALO, :] = jnp.where(i == 0, 0.0, prev_ref[...])
    ext_ref[POOL_HALO:POOL_HALO + tm, :] = x_ref[...]
    ext_ref[POOL_HALO + tm:, :] = jnp.where(i == tiles_per_seq - 1, 0.0, next_ref[...])
    pos = i * tm + lax.broadcasted_iota(I32, (tm, 1), 0)
    group = d // len(POOL_WINDOWS)
    for gi, win in enumerate(POOL_WINDOWS):
        half = win // 2
        cols = slice(gi * group, (gi + 1) * group)
        wsum = ext_ref[POOL_HALO - half:POOL_HALO - half + tm, cols]
        for j in range(1 - half, half):
            wsum = wsum + ext_ref[POOL_HALO + j:POOL_HALO + j + tm, cols]
        count = (jnp.minimum(pos + half, seq) - jnp.maximum(pos - half, 0)).astype(F32)
        dev = wsum / count - x_ref[:, cols]
        mixed = jnp.dot(dev.astype(BF16), pw_ref[gi].astype(BF16), preferred_element_type=F32)
        mix_ref[:, cols] = mixed * ps_ref[:, cols]
    _ln_router_rows(alpha, n_experts, slice(0, tm), x_ref, mix_ref, g_ref, b_ref, rw_ref, xe_ref, afft_ref)


def _pool_ln_router(x, pool_w, pool_scale, g, b, rw_pad, alpha, n_experts, seq, tm=256):
    t, d = x.shape
    r8 = tm // SUBLANES
    last8 = t // SUBLANES - 1
    const = lambda a: pl.BlockSpec(a.shape, lambda i: (0,) * a.ndim)
    return pl.pallas_call(
        functools.partial(_pool_ln_router_kernel, alpha, n_experts, seq),
        grid=(t // tm,),
        in_specs=[
            pl.BlockSpec((tm, d), lambda i: (i, 0)),
            pl.BlockSpec((SUBLANES, d), lambda i: (jnp.maximum(i * r8 - 1, 0), 0)),
            pl.BlockSpec((SUBLANES, d), lambda i: (jnp.minimum((i + 1) * r8, last8), 0)),
            const(pool_w), const(pool_scale), const(g), const(b), const(rw_pad),
        ],
        out_specs=[pl.BlockSpec((tm, d + LANES), lambda i: (i, 0)), pl.BlockSpec((n_experts, tm), lambda i: (0, i))],
        out_shape=[jax.ShapeDtypeStruct((t, d + LANES), F32), jax.ShapeDtypeStruct((n_experts, t), F32)],
        scratch_shapes=[pltpu.VMEM((tm + 2 * POOL_HALO, d), F32), pltpu.VMEM((tm, d), F32)],
        compiler_params=_cparams("arbitrary"),
        name="pool_ln_router",
    )(x, x, x, pool_w, pool_scale, g, b, rw_pad)


MOE_TILE = 256
MOE_TAB = 32
MOE_JBLK = 128
MOE_WIN = 16


def _lane_cumsum(mask_f, tri):
    rows, n = mask_f.shape
    run = jnp.zeros((rows, 1), F32)
    parts, starts = [], []
    for c in range(n // MOE_TILE):
        starts.append(run)
        m = mask_f[:, c * MOE_TILE:(c + 1) * MOE_TILE].astype(BF16)
        cs = jnp.dot(m, tri, preferred_element_type=F32) + run
        parts.append(cs)
        run = cs[:, MOE_TILE - 1:MOE_TILE]
    starts.append(run)
    return jnp.concatenate(parts, axis=1), starts


def _moe_select_kernel(cap, afft_ref, idx_ref, tok_ref, tab_ref, csel_ref):
    n_exp, seq = afft_ref.shape
    aff = afft_ref[...]

    def search(it, thr_bits):
        cand = thr_bits | jnp.left_shift(jnp.int32(1), 30 - it)
        cnt = jnp.sum(jnp.where(aff >= pltpu.bitcast(cand, F32), 1.0, 0.0), axis=1, keepdims=True)
        return jnp.where(cnt >= cap, cand, thr_bits)

    thr = pltpu.bitcast(lax.fori_loop(0, 31, search, jnp.zeros((n_exp, 1), I32)), F32)
    gt = aff > thr
    eq = aff == thr
    need = cap - jnp.sum(jnp.where(gt, 1.0, 0.0), axis=1, keepdims=True)
    r = lax.broadcasted_iota(I32, (MOE_TILE, MOE_TILE), 0)
    c = lax.broadcasted_iota(I32, (MOE_TILE, MOE_TILE), 1)
    tri = jnp.where(r <= c, 1.0, 0.0).astype(BF16)
    ceq, _ = _lane_cumsum(jnp.where(eq, 1.0, 0.0), tri)
    sel = jnp.where(gt, 1.0, jnp.where(eq & (ceq <= need), 1.0, 0.0))
    csel, starts = _lane_cumsum(sel, tri)
    csel_ref[...] = csel
    lane = lax.broadcasted_iota(I32, (n_exp, MOE_TAB), 1)
    tab = jnp.zeros((n_exp, MOE_TAB), I32)
    for ti, s in enumerate(starts):
        tab = jnp.where(lane == ti, s.astype(I32), tab)
    tab_ref[...] = tab

    def per_expert(e, carry):
        row = csel_ref[pl.ds(e, 1), :]
        for jb in range(cap // MOE_JBLK):
            jcol = (lax.broadcasted_iota(I32, (MOE_JBLK, LANES), 0) + jb * MOE_JBLK).astype(F32)
            acc = jnp.zeros((MOE_JBLK, LANES), F32)
            for tc in range(seq // LANES):
                acc = acc + jnp.where(row[:, tc * LANES:(tc + 1) * LANES] <= jcol, 1.0, 0.0)
            tokcol = jnp.sum(acc, axis=1, keepdims=True).astype(I32)
            tokb = jnp.broadcast_to(tokcol, (MOE_JBLK, LANES))
            tok_ref[pl.ds(pl.multiple_of(e * cap + jb * MOE_JBLK, MOE_JBLK), MOE_JBLK), :] = tokb
            idx_ref[e, :, jb * MOE_JBLK:(jb + 1) * MOE_JBLK] = jnp.transpose(tokb)[0:1, :]
        return carry

    lax.fori_loop(0, n_exp, per_expert, 0)


def _moe_select(afft, batch, seq, cap):
    n_exp = afft.shape[0]
    assert seq // MOE_TILE + 1 <= MOE_TAB
    return pl.pallas_call(
        functools.partial(_moe_select_kernel, cap),
        grid=(batch,),
        in_specs=[pl.BlockSpec((n_exp, seq), lambda b: (0, b))],
        out_specs=[
            pl.BlockSpec((n_exp, None, 1, cap), lambda b: (0, b, 0, 0)),
            pl.BlockSpec((n_exp * cap, LANES), lambda b: (b, 0)),
            pl.BlockSpec((n_exp, MOE_TAB), lambda b: (b, 0)),
        ],
        out_shape=[
            jax.ShapeDtypeStruct((n_exp, batch, 1, cap), I32),
            jax.ShapeDtypeStruct((batch * n_exp * cap, LANES), I32),
            jax.ShapeDtypeStruct((batch * n_exp, MOE_TAB), I32),
        ],
        scratch_shapes=[pltpu.VMEM((n_exp, seq), F32)],
        compiler_params=_cparams("arbitrary"),
        name="moe_select",
    )(afft)


MOE_PAIR = 2


MOE_CONV_ROWS = 64
MOE_EXPERT_FF_TILE = 512
MOE_EXPERT_VMEM_LIMIT_BYTES = 61 * 1024 * 1024


def _moe_expert_kernel(seq, cap, cur, nxt, x_hbm, w1_ref, w3_ref, w2_ref, y_ref,
                       xs, xb, gate, acc, w1b, w3b, w2b, sem):
    e, h, f = pl.program_id(0), pl.program_id(1), pl.program_id(2)
    n_exp, n_half, nf = pl.num_programs(0), pl.num_programs(1), pl.num_programs(2)
    rows = MOE_PAIR * cap
    d = xb.shape[1]
    step = e * n_half + h
    last_step = n_exp * n_half - 1

    def start_row(idx_ref, half, s, pos):
        src = (half * MOE_PAIR + s) * seq + idx_ref[s, 0, pos]
        pltpu.make_async_copy(x_hbm.at[pl.ds(src, 1)], xs.at[pl.ds(s * cap + pos, 1)], sem.at[0]).start()

    def wait_rows():
        pltpu.make_async_copy(x_hbm.at[pl.ds(0, rows)], xs, sem.at[0]).wait()

    @pl.when((step == 0) & (f == 0))
    def _():
        for s in range(MOE_PAIR):
            lax.fori_loop(0, cap, lambda p, c: (start_row(cur, h, s, p), c)[1], 0)

    @pl.when(f == 0)
    def _():
        wait_rows()
        lane = lax.broadcasted_iota(I32, (MOE_CONV_ROWS, LANES), 1)
        for r in range(0, rows, MOE_CONV_ROWS):
            x = xs[r:r + MOE_CONV_ROWS, :]
            xb[r:r + MOE_CONV_ROWS, :] = x[:, :d].astype(BF16)
            gate[r:r + MOE_CONV_ROWS, :] = jnp.sum(jnp.where(lane == e, x[:, d:], 0.0), axis=1, keepdims=True)
        acc[...] = jnp.zeros_like(acc)

    per = rows // nf
    assert cap % per == 0
    nxt_h = jnp.minimum(step + 1, last_step) % n_half
    nxt_s = f // (cap // per)
    nxt_pos0 = (f % (cap // per)) * per
    for j in range(per):
        start_row(nxt, nxt_h, nxt_s, nxt_pos0 + j)

    for r in range(0, d, DOT_TM):
        w1b[r:r + DOT_TM, :] = w1_ref[r:r + DOT_TM, :].astype(BF16)
        w3b[r:r + DOT_TM, :] = w3_ref[r:r + DOT_TM, :].astype(BF16)
    for c in range(0, d, DOT_TN):
        w2b[:, c:c + DOT_TN] = w2_ref[:, c:c + DOT_TN].astype(BF16)

    for r in range(0, rows, DOT_TM):
        rs = slice(r, r + DOT_TM)
        xv = xb[rs, :]
        h1 = jnp.dot(xv, w1b[...], preferred_element_type=F32)
        h3 = jnp.dot(xv, w3b[...], preferred_element_type=F32)
        hh = (h1 / (1.0 + jnp.exp(-h1)) * h3).astype(BF16)
        for c in range(0, d, DOT_TN):
            acc[rs, c:c + DOT_TN] += jnp.dot(hh, w2b[:, c:c + DOT_TN], preferred_element_type=F32)

    @pl.when(f == nf - 1)
    def _():
        for r in range(0, rows, DOT_TM):
            y_ref[r:r + DOT_TM, :] = (acc[r:r + DOT_TM, :] * gate[r:r + DOT_TM, :]).astype(y_ref.dtype)

        @pl.when(step == last_step)
        def _():
            wait_rows()


def _moe_experts(x1e, idx, w1, w3, w2, layer, batch, seq, cap, tf):
    _, n_exp, d, ff = w1.shape
    n_half = batch // MOE_PAIR
    rows = MOE_PAIR * cap
    width = x1e.shape[1]

    def nxt(e, h):
        s = jnp.minimum(e * n_half + h + 1, n_exp * n_half - 1)
        return s // n_half, s % n_half

    idx_spec = lambda fn: pl.BlockSpec((None, MOE_PAIR, 1, cap), fn, memory_space=pltpu.SMEM)
    return pl.pallas_call(
        functools.partial(_moe_expert_kernel, seq, cap),
        grid=(n_exp, n_half, ff // tf),
        in_specs=[
            idx_spec(lambda e, h, f: (e, h, 0, 0)),
            idx_spec(lambda e, h, f: nxt(e, h) + (0, 0)),
            pl.BlockSpec(memory_space=pl.ANY),
            pl.BlockSpec((None, None, d, tf), lambda e, h, f: (layer, e, 0, f)),
            pl.BlockSpec((None, None, d, tf), lambda e, h, f: (layer, e, 0, f)),
            pl.BlockSpec((None, None, tf, d), lambda e, h, f: (layer, e, f, 0)),
        ],
        out_specs=pl.BlockSpec((rows, d), lambda e, h, f: (e * n_half + h, 0), pipeline_mode=pl.Buffered(1)),
        out_shape=jax.ShapeDtypeStruct((n_exp * batch * cap, d), BF16),
        scratch_shapes=[
            pltpu.VMEM((rows, width), F32),
            pltpu.VMEM((rows, d), BF16),
            pltpu.VMEM((rows, 1), F32),
            pltpu.VMEM((rows, d), F32),
            pltpu.VMEM((d, tf), BF16),
            pltpu.VMEM((d, tf), BF16),
            pltpu.VMEM((tf, d), BF16),
            pltpu.SemaphoreType.DMA((1,)),
        ],
        compiler_params=pltpu.CompilerParams(dimension_semantics=("arbitrary",) * 3,
                                             vmem_limit_bytes=MOE_EXPERT_VMEM_LIMIT_BYTES),
        name="moe_experts",
    )(idx, idx, x1e, w1, w3, w2)


def _moe_combine_kernel(alpha, n_exp, batch, cap, tab_ref, y_hbm, tok_hbm, x_ref, g_ref, b_ref, o_ref,
                        ybuf, tokbuf, acc, nwin_ref, sem):
    b, i = pl.program_id(0), pl.program_id(1)
    n_tiles = pl.num_programs(1)
    tm = x_ref.shape[0]
    step = b * n_tiles + i
    slot = step % 2

    def window_copies(src_y, src_t, buf, off):
        dst = pl.ds(pl.multiple_of(off * MOE_WIN, MOE_WIN), MOE_WIN)
        return (pltpu.make_async_copy(y_hbm.at[pl.ds(pl.multiple_of(src_y, MOE_WIN), MOE_WIN)],
                                      ybuf.at[buf, dst], sem.at[buf, 0]),
                pltpu.make_async_copy(tok_hbm.at[pl.ds(pl.multiple_of(src_t, MOE_WIN), MOE_WIN)],
                                      tokbuf.at[buf, dst], sem.at[buf, 1]))

    def fetch_tile(bb, ii, buf):
        def per_expert(e, off):
            base = (bb * n_exp + e) * MOE_TAB + ii
            s0, s1 = tab_ref[base], tab_ref[base + 1]
            w0 = s0 // MOE_WIN
            nw = jnp.where(s1 > s0, (s1 + MOE_WIN - 1) // MOE_WIN - w0, 0)

            def per_window(w, off2):
                for cp in window_copies((e * batch + bb) * cap + (w0 + w) * MOE_WIN,
                                        (bb * n_exp + e) * cap + (w0 + w) * MOE_WIN, buf, off2):
                    cp.start()
                return off2 + 1

            return lax.fori_loop(0, nw, per_window, off)

        nwin_ref[buf] = lax.fori_loop(0, n_exp, per_expert, 0)

    @pl.when(step == 0)
    def _():
        ybuf[...] = jnp.zeros_like(ybuf)
        tokbuf[...] = jnp.zeros_like(tokbuf)
        fetch_tile(b, i, slot)

    nwin = nwin_ref[slot]

    def wait_window(w, c):
        for cp in window_copies(0, 0, slot, w):
            cp.wait()
        return c

    lax.fori_loop(0, nwin, wait_window, 0)

    @pl.when(step + 1 < pl.num_programs(0) * n_tiles)
    def _():
        nxt = step + 1
        fetch_tile(nxt // n_tiles, nxt % n_tiles, 1 - slot)

    nrows = nwin * MOE_WIN
    acc[...] = jnp.zeros_like(acc)
    sub = lax.broadcasted_iota(I32, (MOE_TILE, LANES), 0)
    lane = lax.broadcasted_iota(I32, (MOE_TILE, LANES), 1)

    def chunk(c, carry):
        rows = pl.ds(pl.multiple_of(c * MOE_TILE, MOE_TILE), MOE_TILE)
        t_local = tokbuf[slot, rows, :] - i * tm
        valid = (sub + c * MOE_TILE) < nrows
        halves = [jnp.where(valid & (t_local == lane + k * LANES), 1.0, 0.0) for k in range(tm // LANES)]
        onehot = jnp.transpose(jnp.concatenate(halves, axis=1)).astype(BF16)
        for cc in range(0, acc.shape[1], DOT_TN):
            acc[:, cc:cc + DOT_TN] += jnp.dot(onehot, ybuf[slot, rows, cc:cc + DOT_TN], preferred_element_type=F32)
        return carry

    lax.fori_loop(0, (nrows + MOE_TILE - 1) // MOE_TILE, chunk, 0)
    o_ref[...] = _layer_norm_rows(alpha * x_ref[...] + acc[...], g_ref[...], b_ref[...])


def _moe_combine(x1e, y, tok, tab, g, b, alpha, batch, seq, cap, n_exp):
    t = x1e.shape[0]
    d = y.shape[1]
    tm = MOE_TILE
    n_tiles = seq // tm
    max_rows = n_exp * (tm + MOE_WIN)
    max_rows = (max_rows + MOE_TILE - 1) // MOE_TILE * MOE_TILE
    return pl.pallas_call(
        functools.partial(_moe_combine_kernel, alpha, n_exp, batch, cap),
        grid_spec=pltpu.PrefetchScalarGridSpec(
            num_scalar_prefetch=1,
            grid=(batch, n_tiles),
            in_specs=[
                pl.BlockSpec(memory_space=pl.ANY),
                pl.BlockSpec(memory_space=pl.ANY),
                pl.BlockSpec((tm, d), lambda bb, i, tab_r: (bb * n_tiles + i, 0)),
                pl.BlockSpec((1, d), lambda bb, i, tab_r: (0, 0)),
                pl.BlockSpec((1, d), lambda bb, i, tab_r: (0, 0)),
            ],
            out_specs=pl.BlockSpec((tm, d), lambda bb, i, tab_r: (bb * n_tiles + i, 0)),
            scratch_shapes=[
                pltpu.VMEM((2, max_rows, d), BF16),
                pltpu.VMEM((2, max_rows, LANES), I32),
                pltpu.VMEM((tm, d), F32),
                pltpu.SMEM((2,), I32),
                pltpu.SemaphoreType.DMA((2, 2)),
            ],
        ),
        out_shape=jax.ShapeDtypeStruct((t, d), F32),
        compiler_params=_cparams("arbitrary", "arbitrary"),
        name="moe_combine",
    )(tab.reshape(-1), y, tok, x1e, g, b)


def _moe_block(x1e, afft, g, b, w1, w3, w2, layer, alpha, batch, seq):
    n_exp = w1.shape[1]
    cap = EC_CAPACITY_FACTOR * seq // n_exp
    idx, tok, tab = _moe_select(afft, batch, seq, cap)
    y = _moe_experts(x1e, idx, w1, w3, w2, layer, batch, seq, cap, tf=MOE_EXPERT_FF_TILE)
    return _moe_combine(x1e, y, tok, tab, g, b, alpha, batch, seq, cap, n_exp)


def kernel(x, mix_w_in, mix_w_out, hy_conv_w, hy_conv_b, hy_ffn_w1, hy_ffn_b1, hy_ffn_w2, hy_ffn_b2, hy_ffn_w3,
           hy_ffn_b3, hy_ffn_w4, hy_sin_freq, hy_bias, pool_w, pool_scale, ln_mix_g, ln_mix_b, ln_ffn_g, ln_ffn_b,
           router_w, exp_w1, exp_w3, exp_w2):
    batch, seq, d = x.shape
    depth = ln_mix_g.shape[0]
    alpha = (2 * depth) ** 0.25
    n_exp = router_w.shape[2]
    xt = x.reshape(batch * seq, d)
    row = lambda v: v[None, :]
    for layer in range(depth):
        i = layer // 2
        rw_pad = jnp.pad(router_w[layer], ((0, 0), (0, LANES - n_exp)))
        if layer % 2 == 0:
            proj = _matmul([xt], mix_w_in[i], BF16, 1024, 1024)
            attn = _dilated_attention(proj, batch, seq)
            hcat = _hyena_filters(seq, hy_ffn_w1[i], hy_ffn_b1[i], hy_ffn_w2[i], hy_ffn_b2[i], hy_ffn_w3[i],
                                  hy_ffn_b3[i], hy_ffn_w4[i], hy_sin_freq[i])
            z, x0c = _hyena_pre(proj, hy_conv_w[i], hy_conv_b[i], seq)
            cosm, nsinm = _dft_matrices(seq)
            k_re, k_im = _dft_filter(cosm, nsinm, hcat, 2048)
            p_re, p_im = _dft_fwd(cosm, nsinm, z, k_re, k_im, batch, 2048)
            hyena = _dft_inv(cosm, nsinm, p_re, p_im, z, x0c, hy_bias[i], batch, 1024, 2048)
            x1e, afft = _outproj_ln_router(attn, hyena, mix_w_out[i], xt, row(ln_mix_g[layer]), row(ln_mix_b[layer]),
                                           rw_pad, alpha, n_exp)
        else:
            x1e, afft = _pool_ln_router(xt, pool_w[i], row(pool_scale[i]), row(ln_mix_g[layer]), row(ln_mix_b[layer]),
                                        rw_pad, alpha, n_exp, seq)
        xt = _moe_block(x1e, afft, row(ln_ffn_g[layer]), row(ln_ffn_b[layer]), exp_w1, exp_w3, exp_w2, layer,
                        alpha, batch, seq)
    return xt.reshape(batch, seq, d)
```
